```python
import math
import jax, jax.numpy as jnp
from jax import lax
import numpy as np

D_MODEL = 1024
BATCH = 2
SEQ = 8192
DEPTH = 4

D_CONV = 256
CONV_WIDTH = 3
D_SSM = 256
SSM_GROUP = 16
N_SSM_GROUPS = D_SSM // SSM_GROUP
SSM_STATE = 64
DT_MIN = 1e-3
DT_MAX = 1e-1
N_Q_HEADS = 8
N_KV_HEADS = 2
HEAD_DIM = 64
D_ATTN = N_Q_HEADS * HEAD_DIM
D_KV = N_KV_HEADS * HEAD_DIM
WINDOW = 128
BLOCK = 128
N_BRANCH = 3
D_FF = 2816
ALPHA = (2 * DEPTH) ** 0.25
BETA = (8 * DEPTH) ** -0.25
LN_EPS = 1e-5
D_IN = 3 * D_CONV + D_SSM + D_ATTN + 2 * D_KV + N_BRANCH * D_MODEL

kernel_name = 'hybrid_gated_conv_s5_swa_macaron_deepnorm'


def layer_norm(x, g, b):
    xf = x.astype(jnp.float32)
    mu = jnp.mean(xf, axis=-1, keepdims=True)
    var = jnp.mean(jnp.square(xf - mu), axis=-1, keepdims=True)
    y = (xf - mu) * lax.rsqrt(var + LN_EPS)
    return (y * g.astype(jnp.float32) + b.astype(jnp.float32)).astype(x.dtype)


def swiglu(x, w_gate, w_up, w_down):
    return (jax.nn.silu(x @ w_gate) * (x @ w_up)) @ w_down


def short_conv_mixer(b_gate, c_gate, h, conv_w, conv_b):
    z = c_gate * h
    kern = conv_w[:, None, :].astype(z.dtype)
    y = lax.conv_general_dilated(z, kern, window_strides=(1,), padding=[(CONV_WIDTH - 1, 0)],
                                 dimension_numbers=('NWC', 'WIO', 'NWC'), feature_group_count=D_CONV)
    return b_gate * (y + conv_b)


def s5_mixer(u, a_re, a_im, log_dt, b_re, b_im, c_re, c_im, d_skip, w_glu):
    f32 = jnp.float32
    bsz, s, _ = u.shape
    uf = u.astype(f32).reshape(bsz, s, N_SSM_GROUPS, SSM_GROUP)
    lr = a_re.astype(f32)
    li = a_im.astype(f32)
    dt = jnp.exp(log_dt.astype(f32))[:, None]
    mag = jnp.exp(lr * dt)
    ang = li * dt
    abar_re = mag * jnp.cos(ang)
    abar_im = mag * jnp.sin(ang)
    nr = abar_re - 1.0
    ni = abar_im
    den = lr * lr + li * li
    coef_re = (nr * lr + ni * li) / den
    coef_im = (ni * lr - nr * li) / den
    br = b_re.astype(f32)
    bi = b_im.astype(f32)
    bbar_re = coef_re[..., None] * br - coef_im[..., None] * bi
    bbar_im = coef_re[..., None] * bi + coef_im[..., None] * br
    bu_re = jnp.einsum('bsgc,gpc->bsgp', uf, bbar_re)
    bu_im = jnp.einsum('bsgc,gpc->bsgp', uf, bbar_im)
    aa_re = jnp.broadcast_to(abar_re, bu_re.shape)
    aa_im = jnp.broadcast_to(abar_im, bu_im.shape)

    def combine(e1, e2):
        a1r, a1i, b1r, b1i = e1
        a2r, a2i, b2r, b2i = e2
        return (a2r * a1r - a2i * a1i,
                a2r * a1i + a2i * a1r,
                a2r * b1r - a2i * b1i + b2r,
                a2r * b1i + a2i * b1r + b2i)

    _, _, xs_re, xs_im = lax.associative_scan(combine, (aa_re, aa_im, bu_re, bu_im), axis=1)
    y = (jnp.einsum('bsgp,gcp->bsgc', xs_re, c_re.astype(f32))
         - jnp.einsum('bsgp,gcp->bsgc', xs_im, c_im.astype(f32)))
    y = y + d_skip.astype(f32).reshape(N_SSM_GROUPS, SSM_GROUP) * uf
    y = jax.nn.gelu(y.reshape(bsz, s, D_SSM))
    y = y * jax.nn.sigmoid(y @ w_glu.astype(f32))
    return y.astype(u.dtype)


def sliding_window_attention(q, k, v, sinks):
    f32 = jnp.float32
    bsz, s, _ = q.shape
    nb = s // BLOCK
    grp = N_Q_HEADS // N_KV_HEADS
    qb = q.astype(f32).reshape(bsz, nb, BLOCK, N_KV_HEADS, grp, HEAD_DIM) * (HEAD_DIM ** -0.5)
    kb = k.astype(f32).reshape(bsz, nb, BLOCK, N_KV_HEADS, HEAD_DIM)
    vb = v.astype(f32).reshape(bsz, nb, BLOCK, N_KV_HEADS, HEAD_DIM)
    pad = ((0, 0), (1, 0), (0, 0), (0, 0), (0, 0))
    kk = jnp.concatenate([jnp.pad(kb, pad)[:, :-1], kb], axis=2)
    vv = jnp.concatenate([jnp.pad(vb, pad)[:, :-1], vb], axis=2)
    scores = jnp.einsum('bnqhgd,bnkhd->bnhgqk', qb, kk)
    qpos = jnp.arange(BLOCK)[:, None] + BLOCK
    kpos = jnp.arange(2 * BLOCK)[None, :]
    diff = qpos - kpos
    band = (diff >= 0) & (diff < WINDOW)
    has_prev = (jnp.arange(nb) > 0)[:, None, None] | (kpos >= BLOCK)[None]
    mask = band[None] & has_prev
    scores = jnp.where(mask[None, :, None, None], scores, -jnp.inf)
    sink = sinks.astype(f32).reshape(N_KV_HEADS, grp)[None, None, :, :, None, None]
    m = jnp.maximum(jnp.max(scores, axis=-1, keepdims=True), sink)
    p = jnp.exp(scores - m)
    probs = p / (jnp.sum(p, axis=-1, keepdims=True) + jnp.exp(sink - m))
    out = jnp.einsum('bnhgqk,bnkhd->bnqhgd', probs, vv)
    return out.reshape(bsz, s, D_ATTN).astype(q.dtype)


def hybrid_mixer(x, w_in, conv_w, conv_b, ssm_a_re, ssm_a_im, ssm_log_dt, ssm_b_re, ssm_b_im,
                 ssm_c_re, ssm_c_im, ssm_d, ssm_w_glu, attn_sinks, w_br_conv, w_br_ssm, w_br_attn, w_out):
    bsz, s, _ = x.shape
    proj = x @ w_in
    widths = [D_CONV, D_CONV, D_CONV, D_SSM, D_ATTN, D_KV, D_KV]
    offs = [int(o) for o in np.cumsum(widths)]
    b_g, c_g, h, u, q, k, v, gates = jnp.split(proj, offs, axis=-1)
    y_conv = short_conv_mixer(b_g, c_g, h, conv_w, conv_b) @ w_br_conv
    y_ssm = s5_mixer(u, ssm_a_re, ssm_a_im, ssm_log_dt, ssm_b_re, ssm_b_im,
                     ssm_c_re, ssm_c_im, ssm_d, ssm_w_glu) @ w_br_ssm
    y_attn = sliding_window_attention(q, k, v, attn_sinks) @ w_br_attn
    g = jax.nn.sigmoid(gates).reshape(bsz, s, N_BRANCH, D_MODEL)
    merged = g[:, :, 0] * y_conv + g[:, :, 1] * y_ssm + g[:, :, 2] * y_attn
    return merged @ w_out


def setup_inputs(seed: int = 0) -> dict:
    key = jax.random.key(seed)
    ks = jax.random.split(key, 32)
    L = DEPTH

    def nrm(k, shape, scale):
        return jax.random.normal(k, shape, jnp.float32) * scale

    def gain(k):
        return 1.0 + nrm(k, (L, D_MODEL), 0.01)

    n_idx = jnp.arange(SSM_STATE, dtype=jnp.float32)
    return {
        'x': nrm(ks[0], (BATCH, SEQ, D_MODEL), 1.0),
        'ffn1_w_gate': nrm(ks[1], (L, D_MODEL, D_FF), D_MODEL ** -0.5),
        'ffn1_w_up': nrm(ks[2], (L, D_MODEL, D_FF), D_MODEL ** -0.5),
        'ffn1_w_down': nrm(ks[3], (L, D_FF, D_MODEL), BETA * D_FF ** -0.5),
        'ln1_g': gain(ks[4]),
        'ln1_b': nrm(ks[5], (L, D_MODEL), 0.01),
        'w_in': nrm(ks[6], (L, D_MODEL, D_IN), D_MODEL ** -0.5),
        'conv_w': nrm(ks[7], (L, CONV_WIDTH, D_CONV), CONV_WIDTH ** -0.5),
        'conv_b': nrm(ks[8], (L, D_CONV), 0.01),
        'ssm_a_re': -0.5 + nrm(ks[9], (L, N_SSM_GROUPS, SSM_STATE), 0.01),
        'ssm_a_im': jnp.pi * n_idx + nrm(ks[10], (L, N_SSM_GROUPS, SSM_STATE), 0.01),
        'ssm_log_dt': jax.random.uniform(ks[11], (L, N_SSM_GROUPS), jnp.float32,
                                         math.log(DT_MIN), math.log(DT_MAX)),
        'ssm_b_re': nrm(ks[12], (L, N_SSM_GROUPS, SSM_STATE, SSM_GROUP), (2 * SSM_GROUP) ** -0.5),
        'ssm_b_im': nrm(ks[13], (L, N_SSM_GROUPS, SSM_STATE, SSM_GROUP), (2 * SSM_GROUP) ** -0.5),
        'ssm_c_re': nrm(ks[14], (L, N_SSM_GROUPS, SSM_GROUP, SSM_STATE), SSM_STATE ** -0.5),
        'ssm_c_im': nrm(ks[15], (L, N_SSM_GROUPS, SSM_GROUP, SSM_STATE), SSM_STATE ** -0.5),
        'ssm_d': nrm(ks[16], (L, D_SSM), 1.0),
        'ssm_w_glu': nrm(ks[17], (L, D_SSM, D_SSM), D_SSM ** -0.5),
        'attn_sinks': nrm(ks[18], (L, N_Q_HEADS), 0.5),
        'w_br_conv': nrm(ks[19], (L, D_CONV, D_MODEL), D_CONV ** -0.5),
        'w_br_ssm': nrm(ks[20], (L, D_SSM, D_MODEL), D_SSM ** -0.5),
        'w_br_attn': nrm(ks[21], (L, D_ATTN, D_MODEL), D_ATTN ** -0.5),
        'w_out': nrm(ks[22], (L, D_MODEL, D_MODEL), BETA * D_MODEL ** -0.5),
        'ln2_g': gain(ks[23]),
        'ln2_b': nrm(ks[24], (L, D_MODEL), 0.01),
        'ffn2_w_gate': nrm(ks[25], (L, D_MODEL, D_FF), D_MODEL ** -0.5),
        'ffn2_w_up': nrm(ks[26], (L, D_MODEL, D_FF), D_MODEL ** -0.5),
        'ffn2_w_down': nrm(ks[27], (L, D_FF, D_MODEL), BETA * D_FF ** -0.5),
        'ln3_g': gain(ks[28]),
        'ln3_b': nrm(ks[29], (L, D_MODEL), 0.01),
    }


def reference(x, ffn1_w_gate, ffn1_w_up, ffn1_w_down, ln1_g, ln1_b, w_in, conv_w, conv_b,
              ssm_a_re, ssm_a_im, ssm_log_dt, ssm_b_re, ssm_b_im, ssm_c_re, ssm_c_im, ssm_d,
              ssm_w_glu, attn_sinks, w_br_conv, w_br_ssm, w_br_attn, w_out, ln2_g, ln2_b,
              ffn2_w_gate, ffn2_w_up, ffn2_w_down, ln3_g, ln3_b):
    for l in range(DEPTH):
        x = layer_norm(ALPHA * x + 0.5 * swiglu(x, ffn1_w_gate[l], ffn1_w_up[l], ffn1_w_down[l]),
                       ln1_g[l], ln1_b[l])
        mix = hybrid_mixer(x, w_in[l], conv_w[l], conv_b[l], ssm_a_re[l], ssm_a_im[l], ssm_log_dt[l],
                           ssm_b_re[l], ssm_b_im[l], ssm_c_re[l], ssm_c_im[l], ssm_d[l], ssm_w_glu[l],
                           attn_sinks[l], w_br_conv[l], w_br_ssm[l], w_br_attn[l], w_out[l])
        x = layer_norm(ALPHA * x + mix, ln2_g[l], ln2_b[l])
        x = layer_norm(ALPHA * x + 0.5 * swiglu(x, ffn2_w_gate[l], ffn2_w_up[l], ffn2_w_down[l]),
                       ln3_g[l], ln3_b[l])
    return x
```

```python
import functools

import jax
import jax.numpy as jnp
from jax import lax
from jax.experimental import pallas as pl
from jax.experimental.pallas import tpu as pltpu

D_MODEL = 1024
DEPTH = 4
D_CONV = 256
CONV_WIDTH = 3
D_SSM = 256
SSM_GROUP = 16
N_SSM_GROUPS = D_SSM // SSM_GROUP
SSM_STATE = 64
N_STATE = N_SSM_GROUPS * SSM_STATE
N_Q_HEADS = 8
N_KV_HEADS = 2
Q_PER_KV = N_Q_HEADS // N_KV_HEADS
HEAD_DIM = 64
D_ATTN = N_Q_HEADS * HEAD_DIM
D_KV = N_KV_HEADS * HEAD_DIM
WINDOW = 128
N_BRANCH = 3
D_FF = 2816
ALPHA = (2 * DEPTH) ** 0.25
LN_EPS = 1e-5
D_IN = 3 * D_CONV + D_SSM + D_ATTN + 2 * D_KV + N_BRANCH * D_MODEL

OFF_U = 3 * D_CONV
OFF_Q = OFF_U + D_SSM
OFF_K = OFF_Q + D_ATTN
OFF_V = OFF_K + D_KV
OFF_G = OFF_V + D_KV

SUBLANES = 8
FF_CHUNK = 256
VMEM_LIMIT = 56 * 1024 * 1024

F32 = jnp.float32
BF16 = jnp.bfloat16


def _dot(a, b):
    return jnp.dot(a, b, preferred_element_type=F32)


def _layer_norm(r, g, b):
    mu = jnp.mean(r, axis=-1, keepdims=True)
    d = r - mu
    var = jnp.mean(d * d, axis=-1, keepdims=True)
    return d * lax.rsqrt(var + LN_EPS) * g + b


def _const_spec(shape):
    zeros = (0,) * len(shape)
    return pl.BlockSpec(shape, lambda *_: zeros, pipeline_mode=pl.Buffered(1))


def _params(*sem):
    return pltpu.CompilerParams(dimension_semantics=sem, vmem_limit_bytes=VMEM_LIMIT)


def _ffn_kernel(x_ref, wg_ref, wu_ref, wd_ref, g_ref, b_ref, o_ref):
    x = x_ref[...]
    xb = x.astype(BF16)
    acc = jnp.zeros(x.shape, F32)
    for c in range(D_FF // FF_CHUNK):
        cols = slice(c * FF_CHUNK, (c + 1) * FF_CHUNK)
        gate = _dot(xb, wg_ref[:, cols])
        up = _dot(xb, wu_ref[:, cols])
        h = (jax.nn.silu(gate) * up).astype(BF16)
        acc = acc + _dot(h, wd_ref[cols, :])
    o_ref[...] = _layer_norm(ALPHA * x + 0.5 * acc, g_ref[...], b_ref[...])


def _ffn_ln(x, wg, wu, wd, g, b, tm=512):
    n = x.shape[0]
    row = pl.BlockSpec((tm, D_MODEL), lambda i: (i, 0))
    return pl.pallas_call(
        _ffn_kernel,
        grid=(n // tm,),
        in_specs=[row, _const_spec((D_MODEL, D_FF)), _const_spec((D_MODEL, D_FF)),
                  _const_spec((D_FF, D_MODEL)), _const_spec((1, D_MODEL)), _const_spec((1, D_MODEL))],
        out_specs=row,
        out_shape=jax.ShapeDtypeStruct((n, D_MODEL), F32),
        compiler_params=_params("parallel"),
        name="ffn_ln",
    )(x, wg, wu, wd, g, b)


def _inproj_kernel(x_ref, w_ref, bch_ref, u_ref, q_ref, k_ref, v_ref, gates_ref):
    xb = x_ref[...].astype(BF16)
    bch_ref[...] = _dot(xb, w_ref[:, 0:OFF_U])
    u_ref[...] = _dot(xb, w_ref[:, OFF_U:OFF_Q])
    q_ref[...] = (_dot(xb, w_ref[:, OFF_Q:OFF_K]) * (HEAD_DIM ** -0.5)).astype(BF16)
    k_ref[...] = _dot(xb, w_ref[:, OFF_K:OFF_V]).astype(BF16)
    v_ref[...] = _dot(xb, w_ref[:, OFF_V:OFF_G]).astype(BF16)
    for j in range(N_BRANCH):
        cols = slice(j * D_MODEL, (j + 1) * D_MODEL)
        gates_ref[:, cols] = _dot(xb, w_ref[:, OFF_G + j * D_MODEL:OFF_G + (j + 1) * D_MODEL])


def _inproj(x, w_in, tm=512):
    n = x.shape[0]

    def row(width):
        return pl.BlockSpec((tm, width), lambda i: (i, 0))

    widths = (OFF_U, D_SSM, D_ATTN, D_KV, D_KV, N_BRANCH * D_MODEL)
    dtypes = (F32, F32, BF16, BF16, BF16, F32)
    return pl.pallas_call(
        _inproj_kernel,
        grid=(n // tm,),
        in_specs=[row(D_MODEL), _const_spec((D_MODEL, D_IN))],
        out_specs=[row(w) for w in widths],
        out_shape=[jax.ShapeDtypeStruct((n, w), d) for w, d in zip(widths, dtypes)],
        compiler_params=_params("parallel"),
        name="inproj",
    )(x, w_in)


def _convssm_kernel(bch_ref, u_ref, cw_ref, cb_ref, bbd_ref, tab_ref, cbd_ref, d_ref, wglu_ref,
                    yc_ref, ys_ref, zbuf_ref, st_ref, bu_ref, xs_ref):
    tt = u_ref.shape[0]

    @pl.when(pl.program_id(1) == 0)
    def _():
        zbuf_ref[0:SUBLANES, :] = jnp.zeros((SUBLANES, D_CONV), F32)
        st_ref[...] = jnp.zeros(st_ref.shape, F32)

    b_gate = bch_ref[:, 0:D_CONV]
    z = bch_ref[:, D_CONV:2 * D_CONV] * bch_ref[:, 2 * D_CONV:3 * D_CONV]
    zbuf_ref[SUBLANES:SUBLANES + tt, :] = z
    z1 = zbuf_ref[SUBLANES - 1:SUBLANES - 1 + tt, :]
    z2 = zbuf_ref[SUBLANES - 2:SUBLANES - 2 + tt, :]
    y = cw_ref[0:1, :] * z2 + cw_ref[1:2, :] * z1 + cw_ref[2:3, :] * z + cb_ref[...]
    yc_ref[...] = (b_gate * y).astype(BF16)
    zbuf_ref[0:SUBLANES, :] = zbuf_ref[tt:tt + SUBLANES, :]

    u = u_ref[...]
    bu_ref[...] = _dot(u.astype(BF16), bbd_ref[...])

    def tab(i):
        return tab_ref[i * SUBLANES:(i + 1) * SUBLANES, :]

    def block(r, carry):
        cr, ci = carry
        rows = pl.ds(pl.multiple_of(r * SUBLANES, SUBLANES), SUBLANES)
        xr = bu_ref[rows, 0:N_STATE]
        xi = bu_ref[rows, N_STATE:2 * N_STATE]
        for s, shift in enumerate((1, 2, 4)):
            ar, ai = tab(2 * s), tab(2 * s + 1)
            sr = pltpu.roll(xr, shift, 0)
            si = pltpu.roll(xi, shift, 0)
            xr, xi = xr + ar * sr - ai * si, xi + ar * si + ai * sr
        pr, pi = tab(6), tab(7)
        xr, xi = xr + pr * cr - pi * ci, xi + pr * ci + pi * cr
        xs_ref[rows, 0:N_STATE] = xr
        xs_ref[rows, N_STATE:2 * N_STATE] = xi
        last = slice(SUBLANES - 1, SUBLANES)
        return (jnp.broadcast_to(xr[last, :], xr.shape), jnp.broadcast_to(xi[last, :], xi.shape))

    cr, ci = lax.fori_loop(0, tt // SUBLANES, block, (st_ref[0], st_ref[1]))
    st_ref[0] = cr
    st_ref[1] = ci

    y = _dot(xs_ref[...].astype(BF16), cbd_ref[...]) + d_ref[...] * u
    y = jax.nn.gelu(y)
    y = y * jax.nn.sigmoid(_dot(y.astype(BF16), wglu_ref[...]))
    ys_ref[...] = y.astype(BF16)


def _convssm(bch, u, cw, cb, bbd, tab, cbd, d, wglu, batch, tt=256):
    n = u.shape[0]
    per_b = n // batch // tt

    def row(width):
        return pl.BlockSpec((tt, width), lambda b, t: (b * per_b + t, 0))

    return pl.pallas_call(
        _convssm_kernel,
        grid=(batch, per_b),
        in_specs=[row(3 * D_CONV), row(D_SSM), _const_spec(cw.shape), _const_spec(cb.shape),
                  _const_spec(bbd.shape), _const_spec(tab.shape), _const_spec(cbd.shape),
                  _const_spec(d.shape), _const_spec(wglu.shape)],
        out_specs=[row(D_CONV), row(D_SSM)],
        out_shape=[jax.ShapeDtypeStruct((n, D_CONV), BF16), jax.ShapeDtypeStruct((n, D_SSM), BF16)],
        scratch_shapes=[pltpu.VMEM((tt + SUBLANES, D_CONV), F32),
                        pltpu.VMEM((2, SUBLANES, N_STATE), F32),
                        pltpu.VMEM((tt, 2 * N_STATE), F32),
                        pltpu.VMEM((tt, 2 * N_STATE), F32)],
        compiler_params=_params("arbitrary", "arbitrary"),
        name="conv_ssm",
    )(bch, u, cw, cb, bbd, tab, cbd, d, wglu)


def _attn_kernel(sink_ref, q_ref, kc_ref, vc_ref, kp_ref, vp_ref, o_ref):
    tq = q_ref.shape[0]
    nblk = tq // WINDOW
    has_prev_tile = pl.program_id(1) > 0
    row = lax.broadcasted_iota(jnp.int32, (WINDOW, WINDOW), 0)
    col = lax.broadcasted_iota(jnp.int32, (WINDOW, WINDOW), 1)
    cur_mask = col <= row
    prev_band = col > row
    dn = (((1,), (1,)), ((), ()))
    for j in range(nblk):
        rows = slice(j * WINDOW, (j + 1) * WINDOW)
        q = q_ref[rows, :]
        if j == 0:
            k_prev, v_prev = kp_ref[...], vp_ref[...]
            prev_mask = jnp.logical_and(prev_band, has_prev_tile)
        else:
            prows = slice((j - 1) * WINDOW, j * WINDOW)
            k_prev, v_prev = kc_ref[prows, :], vc_ref[prows, :]
            prev_mask = prev_band
        k_cur, v_cur = kc_ref[rows, :], vc_ref[rows, :]
        outs = []
        for h in range(N_KV_HEADS):
            hs = slice(h * HEAD_DIM, (h + 1) * HEAD_DIM)
            qh = jnp.concatenate(
                [q[:, (h * Q_PER_KV + g) * HEAD_DIM:(h * Q_PER_KV + g + 1) * HEAD_DIM] for g in range(Q_PER_KV)],
                axis=0)
            s_prev = lax.dot_general(qh, k_prev[:, hs], dn, preferred_element_type=F32)
            s_cur = lax.dot_general(qh, k_cur[:, hs], dn, preferred_element_type=F32)
            p_prev, p_cur, denoms = [], [], []
            for g in range(Q_PER_KV):
                gr = slice(g * WINDOW, (g + 1) * WINDOW)
                sink = sink_ref[h * Q_PER_KV + g]
                sp = jnp.where(prev_mask, s_prev[gr], -jnp.inf)
                sc = jnp.where(cur_mask, s_cur[gr], -jnp.inf)
                m = jnp.maximum(jnp.max(jnp.maximum(sp, sc), axis=-1, keepdims=True), sink)
                pp = jnp.exp(sp - m)
                pc = jnp.exp(sc - m)
                denoms.append(jnp.sum(pp + pc, axis=-1, keepdims=True) + jnp.exp(sink - m))
                p_prev.append(pp.astype(BF16))
                p_cur.append(pc.astype(BF16))
            o = (_dot(jnp.concatenate(p_prev, axis=0), v_prev[:, hs])
                 + _dot(jnp.concatenate(p_cur, axis=0), v_cur[:, hs]))
            for g in range(Q_PER_KV):
                gr = slice(g * WINDOW, (g + 1) * WINDOW)
                outs.append(o[gr] / denoms[g])
        o_ref[rows, :] = jnp.concatenate(outs, axis=-1).astype(BF16)


def _attention(sinks, q, k, v, batch, tq=512):
    n = q.shape[0]
    per_b = n // batch // tq
    blk_per_tile = tq // WINDOW

    def row(width):
        return pl.BlockSpec((tq, width), lambda b, t: (b * per_b + t, 0))

    def prev_index(b, t):
        return (jnp.maximum((b * per_b + t) * blk_per_tile - 1, 0), 0)

    prev = pl.BlockSpec((WINDOW, D_KV), prev_index)
    return pl.pallas_call(
        _attn_kernel,
        grid=(batch, per_b),
        in_specs=[pl.BlockSpec(memory_space=pltpu.SMEM), row(D_ATTN), row(D_KV), row(D_KV), prev, prev],
        out_specs=row(D_ATTN),
        out_shape=jax.ShapeDtypeStruct((n, D_ATTN), BF16),
        compiler_params=_params("parallel", "parallel"),
        name="swa",
    )(sinks, q, k, v, k, v)


def _merge_kernel(x_ref, yc_ref, ys_ref, ya_ref, gates_ref, wc_ref, ws_ref, wa_ref, wo_ref, g_ref, b_ref, o_ref):
    merged = None
    for j, (y_ref, w_ref) in enumerate(((yc_ref, wc_ref), (ys_ref, ws_ref), (ya_ref, wa_ref))):
        gate = jax.nn.sigmoid(gates_ref[:, j * D_MODEL:(j + 1) * D_MODEL])
        term = gate * _dot(y_ref[...], w_ref[...])
        merged = term if merged is None else merged + term
    mix = _dot(merged.astype(BF16), wo_ref[...])
    o_ref[...] = _layer_norm(ALPHA * x_ref[...] + mix, g_ref[...], b_ref[...])


def _merge(x, yc, ys, ya, gates, wc, ws, wa, wo, g, b, tm=512):
    n = x.shape[0]

    def row(width):
        return pl.BlockSpec((tm, width), lambda i: (i, 0))

    return pl.pallas_call(
        _merge_kernel,
        grid=(n // tm,),
        in_specs=[row(D_MODEL), row(D_CONV), row(D_SSM), row(D_ATTN), row(N_BRANCH * D_MODEL),
                  _const_spec(wc.shape), _const_spec(ws.shape), _const_spec(wa.shape), _const_spec(wo.shape),
                  _const_spec((1, D_MODEL)), _const_spec((1, D_MODEL))],
        out_specs=row(D_MODEL),
        out_shape=jax.ShapeDtypeStruct((n, D_MODEL), F32),
        compiler_params=_params("parallel"),
        name="merge_ln",
    )(x, yc, ys, ya, gates, wc, ws, wa, wo, g, b)


def _ssm_tables(a_re, a_im, log_dt, b_re, b_im, c_re, c_im):
    dt = jnp.exp(log_dt)[:, None]
    mag = jnp.exp(a_re * dt)
    ang = a_im * dt
    abar_re = mag * jnp.cos(ang)
    abar_im = mag * jnp.sin(ang)
    nr = abar_re - 1.0
    ni = abar_im
    den = a_re * a_re + a_im * a_im
    coef_re = (nr * a_re + ni * a_im) / den
    coef_im = (ni * a_re - nr * a_im) / den
    bbar_re = coef_re[..., None] * b_re - coef_im[..., None] * b_im
    bbar_im = coef_re[..., None] * b_im + coef_im[..., None] * b_re

    eye = jnp.eye(N_SSM_GROUPS, dtype=F32)

    def in_blockdiag(w):
        return jnp.einsum('gpc,gh->gchp', w, eye).reshape(D_SSM, N_STATE)

    def out_blockdiag(w):
        return jnp.einsum('gcp,gh->gphc', w, eye).reshape(N_STATE, D_SSM)

    bbd = jnp.concatenate([in_blockdiag(bbar_re), in_blockdiag(bbar_im)], axis=1).astype(BF16)
    cbd = jnp.concatenate([out_blockdiag(c_re), -out_blockdiag(c_im)], axis=0).astype(BF16)

    ar, ai = abar_re.reshape(1, N_STATE), abar_im.reshape(1, N_STATE)
    pw = [(ar, ai)]
    for _ in range(SUBLANES - 1):
        pr, pi = pw[-1]
        pw.append((pr * ar - pi * ai, pr * ai + pi * ar))
    rows = jnp.arange(SUBLANES)[:, None]
    tabs = []
    for shift in (1, 2, 4):
        pr, pi = pw[shift - 1]
        tabs += [jnp.where(rows >= shift, pr, 0.0), jnp.where(rows >= shift, pi, 0.0)]
    tabs += [jnp.concatenate([p[0] for p in pw], axis=0), jnp.concatenate([p[1] for p in pw], axis=0)]
    return bbd, jnp.concatenate(tabs, axis=0), cbd


def kernel(x, ffn1_w_gate, ffn1_w_up, ffn1_w_down, ln1_g, ln1_b, w_in, conv_w, conv_b, ssm_a_re, ssm_a_im, ssm_log_dt, ssm_b_re, ssm_b_im, ssm_c_re, ssm_c_im, ssm_d, ssm_w_glu, attn_sinks, w_br_conv, w_br_ssm, w_br_attn, w_out, ln2_g, ln2_b, ffn2_w_gate, ffn2_w_up, ffn2_w_down, ln3_g, ln3_b):
    batch, seq, _ = x.shape
    h = x.reshape(batch * seq, D_MODEL)

    def vec(p, l):
        return p[l].reshape(1, -1)

    for l in range(DEPTH):
        h = _ffn_ln(h, ffn1_w_gate[l].astype(BF16), ffn1_w_up[l].astype(BF16), ffn1_w_down[l].astype(BF16),
                    vec(ln1_g, l), vec(ln1_b, l))
        bch, u, q, k, v, gates = _inproj(h, w_in[l].astype(BF16))
        bbd, tab, cbd = _ssm_tables(ssm_a_re[l], ssm_a_im[l], ssm_log_dt[l], ssm_b_re[l], ssm_b_im[l],
                                    ssm_c_re[l], ssm_c_im[l])
        yc, ys = _convssm(bch, u, conv_w[l], vec(conv_b, l), bbd, tab, cbd, vec(ssm_d, l),
                          ssm_w_glu[l].astype(BF16), batch)
        ya = _attention(attn_sinks[l], q, k, v, batch)
        h = _merge(h, yc, ys, ya, gates, w_br_conv[l].astype(BF16), w_br_ssm[l].astype(BF16),
                   w_br_attn[l].astype(BF16), w_out[l].astype(BF16), vec(ln2_g, l), vec(ln2_b, l))
        h = _ffn_ln(h, ffn2_w_gate[l].astype(BF16), ffn2_w_up[l].astype(BF16), ffn2_w_down[l].astype(BF16),
                    vec(ln3_g, l), vec(ln3_b, l))
    return h.reshape(batch, seq, D_MODEL)
```

```python
import functools

import jax
import jax.numpy as jnp
from jax import lax
from jax.experimental import pallas as pl
from jax.experimental.pallas import tpu as pltpu

D_MODEL = 1024
DEPTH = 4
D_CONV = 256
CONV_WIDTH = 3
D_SSM = 256
SSM_GROUP = 16
N_SSM_GROUPS = D_SSM // SSM_GROUP
SSM_STATE = 64
N_STATE = N_SSM_GROUPS * SSM_STATE
N_Q_HEADS = 8
N_KV_HEADS = 2
Q_PER_KV = N_Q_HEADS // N_KV_HEADS
HEAD_DIM = 64
D_ATTN = N_Q_HEADS * HEAD_DIM
D_KV = N_KV_HEADS * HEAD_DIM
WINDOW = 128
N_BRANCH = 3
D_FF = 2816
ALPHA = (2 * DEPTH) ** 0.25
LN_EPS = 1e-5
D_IN = 3 * D_CONV + D_SSM + D_ATTN + 2 * D_KV + N_BRANCH * D_MODEL

OFF_U = 3 * D_CONV
OFF_Q = OFF_U + D_SSM
OFF_K = OFF_Q + D_ATTN
OFF_V = OFF_K + D_KV
OFF_G = OFF_V + D_KV

SUBLANES = 8
LANES = 128
FF_CHUNK = 256
VMEM_LIMIT = 56 * 1024 * 1024

SSM_TILE = 256
SSM_STEPS = SSM_TILE // SUBLANES
N_CHUNKS = N_STATE // LANES
CHUNK_COLS = 2 * LANES

F32 = jnp.float32
BF16 = jnp.bfloat16


def _dot(a, b):
    return jnp.dot(a, b, preferred_element_type=F32)


def _layer_norm(r, g, b):
    mu = jnp.mean(r, axis=-1, keepdims=True)
    d = r - mu
    var = jnp.mean(d * d, axis=-1, keepdims=True)
    return d * lax.rsqrt(var + LN_EPS) * g + b


def _layer_spec(arr, layer):
    zeros = (0,) * (arr.ndim - 1)
    return pl.BlockSpec((None,) + arr.shape[1:], lambda *_: (layer,) + zeros, pipeline_mode=pl.Buffered(1))


def _params(*sem):
    return pltpu.CompilerParams(dimension_semantics=sem, vmem_limit_bytes=VMEM_LIMIT)


def _ffn_kernel(x_ref, wg_ref, wu_ref, wd_ref, g_ref, b_ref, o_ref):
    x = x_ref[...]
    xb = x.astype(BF16)
    acc = jnp.zeros(x.shape, F32)
    for c in range(D_FF // FF_CHUNK):
        cols = slice(c * FF_CHUNK, (c + 1) * FF_CHUNK)
        gate = _dot(xb, wg_ref[:, cols])
        up = _dot(xb, wu_ref[:, cols])
        h = (jax.nn.silu(gate) * up).astype(BF16)
        acc = acc + _dot(h, wd_ref[cols, :])
    o_ref[...] = _layer_norm(ALPHA * x + 0.5 * acc, g_ref[...], b_ref[...])


def _ffn_ln(x, layer, wg, wu, wd, g, b, tm=512):
    n = x.shape[0]
    row = pl.BlockSpec((tm, D_MODEL), lambda i: (i, 0))
    weights = (wg, wu, wd, g, b)
    return pl.pallas_call(
        _ffn_kernel,
        grid=(n // tm,),
        in_specs=[row] + [_layer_spec(w, layer) for w in weights],
        out_specs=row,
        out_shape=jax.ShapeDtypeStruct((n, D_MODEL), F32),
        compiler_params=_params("parallel"),
        name="ffn_ln",
    )(x, *weights)


def _inproj_kernel(x_ref, w_ref, bch_ref, u_ref, q_ref, k_ref, v_ref, gates_ref):
    xb = x_ref[...].astype(BF16)
    bch_ref[...] = _dot(xb, w_ref[:, 0:OFF_U])
    u_ref[...] = _dot(xb, w_ref[:, OFF_U:OFF_Q])
    q_ref[...] = (_dot(xb, w_ref[:, OFF_Q:OFF_K]) * (HEAD_DIM ** -0.5)).astype(BF16)
    k_ref[...] = _dot(xb, w_ref[:, OFF_K:OFF_V]).astype(BF16)
    v_ref[...] = _dot(xb, w_ref[:, OFF_V:OFF_G]).astype(BF16)
    for j in range(N_BRANCH):
        cols = slice(j * D_MODEL, (j + 1) * D_MODEL)
        gates_ref[:, cols] = _dot(xb, w_ref[:, OFF_G + j * D_MODEL:OFF_G + (j + 1) * D_MODEL])


def _inproj(x, layer, w_in, tm=512):
    n = x.shape[0]

    def row(width):
        return pl.BlockSpec((tm, width), lambda i: (i, 0))

    widths = (OFF_U, D_SSM, D_ATTN, D_KV, D_KV, N_BRANCH * D_MODEL)
    dtypes = (F32, F32, BF16, BF16, BF16, F32)
    return pl.pallas_call(
        _inproj_kernel,
        grid=(n // tm,),
        in_specs=[row(D_MODEL), _layer_spec(w_in, layer)],
        out_specs=[row(w) for w in widths],
        out_shape=[jax.ShapeDtypeStruct((n, w), d) for w, d in zip(widths, dtypes)],
        compiler_params=_params("parallel"),
        name="inproj",
    )(x, w_in)


def _cmul_add(ar, ai, xr, xi, br, bi):
    return ar * xr - ai * xi + br, ar * xi + ai * xr + bi


def _convssm_kernel(bch_ref, u_ref, cw_ref, cb_ref, bbd_ref, astep_ref, aseg_ref, apow_ref, cbd_ref, d_ref,
                    wglu_ref, yc_ref, ys_ref, zbuf_ref, st_ref, uh_ref, up_ref, xl_ref, xs_ref, yh_ref):
    tt = SSM_TILE

    @pl.when(pl.program_id(1) == 0)
    def _():
        zbuf_ref[0:SUBLANES, :] = jnp.zeros((SUBLANES, D_CONV), F32)
        st_ref[...] = jnp.zeros(st_ref.shape, F32)

    b_gate = bch_ref[:, 0:D_CONV]
    z = bch_ref[:, D_CONV:2 * D_CONV] * bch_ref[:, 2 * D_CONV:3 * D_CONV]
    zbuf_ref[SUBLANES:SUBLANES + tt, :] = z
    z1 = zbuf_ref[SUBLANES - 1:SUBLANES - 1 + tt, :]
    z2 = zbuf_ref[SUBLANES - 2:SUBLANES - 2 + tt, :]
    y = cw_ref[0:1, :] * z2 + cw_ref[1:2, :] * z1 + cw_ref[2:3, :] * z + cb_ref[...]
    yc_ref[...] = (b_gate * y).astype(BF16)
    zbuf_ref[0:SUBLANES, :] = zbuf_ref[tt:tt + SUBLANES, :]

    for h in range(D_SSM // LANES):
        uh_ref[h] = u_ref[:, h * LANES:(h + 1) * LANES]
        for r in range(SSM_STEPS):
            up_ref[r * SUBLANES:(r + 1) * SUBLANES, h * LANES:(h + 1) * LANES] = (
                uh_ref[h, pl.ds(r, SUBLANES, stride=SSM_STEPS), :])
    upb = up_ref[...].astype(BF16)
    first_row = lax.broadcasted_iota(jnp.int32, (SUBLANES, LANES), 0) == 0
    for c in range(N_CHUNKS):
        re = slice(c * CHUNK_COLS, c * CHUNK_COLS + LANES)
        im = slice(c * CHUNK_COLS + LANES, (c + 1) * CHUNK_COLS)
        bu = _dot(upb, bbd_ref[:, c * CHUNK_COLS:(c + 1) * CHUNK_COLS])
        ar, ai = astep_ref[:, re], astep_ref[:, im]
        xr, xi = bu[0:SUBLANES, 0:LANES], bu[0:SUBLANES, LANES:CHUNK_COLS]
        xl_ref[0:SUBLANES, re] = xr
        xl_ref[0:SUBLANES, im] = xi
        for r in range(1, SSM_STEPS):
            rows = slice(r * SUBLANES, (r + 1) * SUBLANES)
            xr, xi = _cmul_add(ar, ai, xr, xi, bu[rows, 0:LANES], bu[rows, LANES:CHUNK_COLS])
            xl_ref[rows, re] = xr
            xl_ref[rows, im] = xi
        sr = jnp.where(first_row, st_ref[:, re], pltpu.roll(xr, 1, 0))
        si = jnp.where(first_row, st_ref[:, im], pltpu.roll(xi, 1, 0))
        for s, shift in enumerate((1, 2, 4)):
            rows = slice(s * SUBLANES, (s + 1) * SUBLANES)
            sr, si = _cmul_add(aseg_ref[rows, re], aseg_ref[rows, im],
                               pltpu.roll(sr, shift, 0), pltpu.roll(si, shift, 0), sr, si)
        rows = slice(3 * SUBLANES, 4 * SUBLANES)
        nr, ni = _cmul_add(aseg_ref[rows, re], aseg_ref[rows, im], sr, si, xr, xi)
        st_ref[:, re] = pltpu.roll(nr, 1, 0)
        st_ref[:, im] = pltpu.roll(ni, 1, 0)
        sr2 = jnp.concatenate([sr, sr], axis=0)
        si2 = jnp.concatenate([si, si], axis=0)
        for k in range(SSM_STEPS // 2):
            rows = slice(2 * k * SUBLANES, 2 * (k + 1) * SUBLANES)
            fr, fi = _cmul_add(apow_ref[rows, re], apow_ref[rows, im], sr2, si2, xl_ref[rows, re], xl_ref[rows, im])
            xs_ref[rows, re] = fr.astype(BF16)
            xs_ref[rows, im] = fi.astype(BF16)

    yp = _dot(xs_ref[...], cbd_ref[...])
    for h in range(D_SSM // LANES):
        for r in range(SSM_STEPS):
            yh_ref[h, pl.ds(r, SUBLANES, stride=SSM_STEPS), :] = (
                yp[r * SUBLANES:(r + 1) * SUBLANES, h * LANES:(h + 1) * LANES])
    y = jnp.concatenate([yh_ref[h] for h in range(D_SSM // LANES)], axis=-1) + d_ref[...] * u_ref[...]
    y = jax.nn.gelu(y)
    y = y * jax.nn.sigmoid(_dot(y.astype(BF16), wglu_ref[...]))
    ys_ref[...] = y.astype(BF16)


def _convssm(bch, u, layer, cw, cb, bbd, astep, aseg, apow, cbd, d, wglu, batch):
    n = u.shape[0]
    tt = SSM_TILE
    per_b = n // batch // tt

    def row(width):
        return pl.BlockSpec((tt, width), lambda b, t: (b * per_b + t, 0))

    weights = (cw, cb, bbd, astep, aseg, apow, cbd, d, wglu)
    return pl.pallas_call(
        _convssm_kernel,
        grid=(batch, per_b),
        in_specs=[row(3 * D_CONV), row(D_SSM)] + [_layer_spec(w, layer) for w in weights],
        out_specs=[row(D_CONV), row(D_SSM)],
        out_shape=[jax.ShapeDtypeStruct((n, D_CONV), BF16), jax.ShapeDtypeStruct((n, D_SSM), BF16)],
        scratch_shapes=[pltpu.VMEM((tt + SUBLANES, D_CONV), F32),
                        pltpu.VMEM((SUBLANES, 2 * N_STATE), F32),
                        pltpu.VMEM((D_SSM // LANES, tt, LANES), F32),
                        pltpu.VMEM((tt, D_SSM), F32),
                        pltpu.VMEM((tt, 2 * N_STATE), F32),
                        pltpu.VMEM((tt, 2 * N_STATE), BF16),
                        pltpu.VMEM((D_SSM // LANES, tt, LANES), F32)],
        compiler_params=_params("arbitrary", "arbitrary"),
        name="conv_ssm",
    )(bch, u, *weights)


def _attn_kernel(sink_ref, q_ref, kc_ref, vc_ref, kp_ref, vp_ref, o_ref, *, layer):
    tq = q_ref.shape[0]
    nblk = tq // WINDOW
    has_prev_tile = pl.program_id(1) > 0
    row = lax.broadcasted_iota(jnp.int32, (WINDOW, WINDOW), 0)
    col = lax.broadcasted_iota(jnp.int32, (WINDOW, WINDOW), 1)
    cur_mask = col <= row
    prev_band = col > row
    dn = (((1,), (1,)), ((), ()))
    for j in range(nblk):
        rows = slice(j * WINDOW, (j + 1) * WINDOW)
        q = q_ref[rows, :]
        if j == 0:
            k_prev, v_prev = kp_ref[...], vp_ref[...]
            prev_mask = jnp.logical_and(prev_band, has_prev_tile)
        else:
            prows = slice((j - 1) * WINDOW, j * WINDOW)
            k_prev, v_prev = kc_ref[prows, :], vc_ref[prows, :]
            prev_mask = prev_band
        k_cur, v_cur = kc_ref[rows, :], vc_ref[rows, :]
        outs = []
        for h in range(N_KV_HEADS):
            hs = slice(h * HEAD_DIM, (h + 1) * HEAD_DIM)
            qh = jnp.concatenate(
                [q[:, (h * Q_PER_KV + g) * HEAD_DIM:(h * Q_PER_KV + g + 1) * HEAD_DIM] for g in range(Q_PER_KV)],
                axis=0)
            s_prev = lax.dot_general(qh, k_prev[:, hs], dn, preferred_element_type=F32)
            s_cur = lax.dot_general(qh, k_cur[:, hs], dn, preferred_element_type=F32)
            p_prev, p_cur, denoms = [], [], []
            for g in range(Q_PER_KV):
                gr = slice(g * WINDOW, (g + 1) * WINDOW)
                sink = sink_ref[layer, h * Q_PER_KV + g]
                sp = jnp.where(prev_mask, s_prev[gr], -jnp.inf)
                sc = jnp.where(cur_mask, s_cur[gr], -jnp.inf)
                m = jnp.maximum(jnp.max(jnp.maximum(sp, sc), axis=-1, keepdims=True), sink)
                pp = jnp.exp(sp - m)
                pc = jnp.exp(sc - m)
                denoms.append(jnp.sum(pp + pc, axis=-1, keepdims=True) + jnp.exp(sink - m))
                p_prev.append(pp.astype(BF16))
                p_cur.append(pc.astype(BF16))
            o = (_dot(jnp.concatenate(p_prev, axis=0), v_prev[:, hs])
                 + _dot(jnp.concatenate(p_cur, axis=0), v_cur[:, hs]))
            for g in range(Q_PER_KV):
                gr = slice(g * WINDOW, (g + 1) * WINDOW)
                outs.append(o[gr] / denoms[g])
        o_ref[rows, :] = jnp.concatenate(outs, axis=-1).astype(BF16)


def _attention(sinks, layer, q, k, v, batch, tq=512):
    n = q.shape[0]
    per_b = n // batch // tq
    blk_per_tile = tq // WINDOW

    def row(width):
        return pl.BlockSpec((tq, width), lambda b, t: (b * per_b + t, 0))

    def prev_index(b, t):
        return (jnp.maximum((b * per_b + t) * blk_per_tile - 1, 0), 0)

    prev = pl.BlockSpec((WINDOW, D_KV), prev_index)
    return pl.pallas_call(
        functools.partial(_attn_kernel, layer=layer),
        grid=(batch, per_b),
        in_specs=[pl.BlockSpec(memory_space=pltpu.SMEM), row(D_ATTN), row(D_KV), row(D_KV), prev, prev],
        out_specs=row(D_ATTN),
        out_shape=jax.ShapeDtypeStruct((n, D_ATTN), BF16),
        compiler_params=_params("parallel", "parallel"),
        name="swa",
    )(sinks, q, k, v, k, v)


def _merge_kernel(x_ref, yc_ref, ys_ref, ya_ref, gates_ref, wc_ref, ws_ref, wa_ref, wo_ref, g_ref, b_ref, o_ref):
    merged = None
    for j, (y_ref, w_ref) in enumerate(((yc_ref, wc_ref), (ys_ref, ws_ref), (ya_ref, wa_ref))):
        gate = jax.nn.sigmoid(gates_ref[:, j * D_MODEL:(j + 1) * D_MODEL])
        term = gate * _dot(y_ref[...], w_ref[...])
        merged = term if merged is None else merged + term
    mix = _dot(merged.astype(BF16), wo_ref[...])
    o_ref[...] = _layer_norm(ALPHA * x_ref[...] + mix, g_ref[...], b_ref[...])


def _merge(x, yc, ys, ya, gates, layer, wc, ws, wa, wo, g, b, tm=512):
    n = x.shape[0]

    def row(width):
        return pl.BlockSpec((tm, width), lambda i: (i, 0))

    weights = (wc, ws, wa, wo, g, b)
    return pl.pallas_call(
        _merge_kernel,
        grid=(n // tm,),
        in_specs=[row(D_MODEL), row(D_CONV), row(D_SSM), row(D_ATTN), row(N_BRANCH * D_MODEL)]
        + [_layer_spec(w, layer) for w in weights],
        out_specs=row(D_MODEL),
        out_shape=jax.ShapeDtypeStruct((n, D_MODEL), F32),
        compiler_params=_params("parallel"),
        name="merge_ln",
    )(x, yc, ys, ya, gates, *weights)


def _chunked(re, im, axis):
    def split(a):
        return a.reshape(a.shape[:axis] + (N_CHUNKS, 1, LANES) + a.shape[axis + 1:])

    both = jnp.concatenate([split(re), split(im)], axis=axis + 1)
    return both.reshape(re.shape[:axis] + (2 * N_STATE,) + re.shape[axis + 1:])


def _ssm_tables(a_re, a_im, log_dt, b_re, b_im, c_re, c_im):
    dt = jnp.exp(log_dt)[..., None]
    mag = jnp.exp(a_re * dt)
    ang = a_im * dt
    abar_re = mag * jnp.cos(ang)
    abar_im = mag * jnp.sin(ang)
    nr = abar_re - 1.0
    ni = abar_im
    den = a_re * a_re + a_im * a_im
    coef_re = (nr * a_re + ni * a_im) / den
    coef_im = (ni * a_re - nr * a_im) / den
    bbar_re = coef_re[..., None] * b_re - coef_im[..., None] * b_im
    bbar_im = coef_re[..., None] * b_im + coef_im[..., None] * b_re

    eye = jnp.eye(N_SSM_GROUPS, dtype=F32)

    def in_blockdiag(w):
        return jnp.einsum('lgpc,gh->lgchp', w, eye).reshape(DEPTH, D_SSM, N_STATE)

    def out_blockdiag(w):
        return jnp.einsum('lgcp,gh->lgphc', w, eye).reshape(DEPTH, N_STATE, D_SSM)

    bbd = _chunked(in_blockdiag(bbar_re), in_blockdiag(bbar_im), axis=2).astype(BF16)
    cbd = _chunked(out_blockdiag(c_re), -out_blockdiag(c_im), axis=1).astype(BF16)

    pr, pi = abar_re.reshape(DEPTH, 1, N_STATE), abar_im.reshape(DEPTH, 1, N_STATE)
    while pr.shape[1] < SSM_STEPS:
        tr, ti = pr[:, -1:], pi[:, -1:]
        pr, pi = (jnp.concatenate([pr, pr * tr - pi * ti], axis=1),
                  jnp.concatenate([pi, pr * ti + pi * tr], axis=1))

    def bcast(a, rows):
        return jnp.broadcast_to(a, (DEPTH, rows, N_STATE))

    astep = _chunked(bcast(pr[:, 0:1], SUBLANES), bcast(pi[:, 0:1], SUBLANES), axis=2)
    apow = _chunked(jnp.repeat(pr, SUBLANES, axis=1), jnp.repeat(pi, SUBLANES, axis=1), axis=2)

    sublane = jnp.arange(SUBLANES)[None, :, None]
    qr, qi = pr[:, -1:], pi[:, -1:]
    seg_re, seg_im = [], []
    for shift in (1, 2, 4):
        seg_re.append(jnp.where(sublane >= shift, bcast(qr, SUBLANES), 0.0))
        seg_im.append(jnp.where(sublane >= shift, bcast(qi, SUBLANES), 0.0))
        qr, qi = qr * qr - qi * qi, 2.0 * qr * qi
    seg_re.append(bcast(pr[:, -1:], SUBLANES))
    seg_im.append(bcast(pi[:, -1:], SUBLANES))
    aseg = _chunked(jnp.concatenate(seg_re, axis=1), jnp.concatenate(seg_im, axis=1), axis=2)
    return bbd, astep, aseg, apow, cbd


def _vec(p):
    return p.reshape(DEPTH, 1, -1)


def _prepare_mixer(p):
    bf = lambda a: a.astype(BF16)
    bbd, astep, aseg, apow, cbd = _ssm_tables(p['ssm_a_re'], p['ssm_a_im'], p['ssm_log_dt'], p['ssm_b_re'],
                                              p['ssm_b_im'], p['ssm_c_re'], p['ssm_c_im'])
    return dict(
        w_in=bf(p['w_in']),
        ssm=(p['conv_w'], _vec(p['conv_b']), bbd, astep, aseg, apow, cbd, _vec(p['ssm_d']), bf(p['ssm_w_glu'])),
        sinks=p['attn_sinks'],
        merge=(bf(p['w_br_conv']), bf(p['w_br_ssm']), bf(p['w_br_attn']), bf(p['w_out']),
               _vec(p['ln2_g']), _vec(p['ln2_b'])))


def _mixer_layer(h, layer, batch, mix):
    bch, u, q, k, v, gates = _inproj(h, layer, mix['w_in'])
    yc, ys = _convssm(bch, u, layer, *mix['ssm'], batch)
    ya = _attention(mix['sinks'], layer, q, k, v, batch)
    return _merge(h, yc, ys, ya, gates, layer, *mix['merge'])


def kernel(x, ffn1_w_gate, ffn1_w_up, ffn1_w_down, ln1_g, ln1_b, w_in, conv_w, conv_b, ssm_a_re, ssm_a_im, ssm_log_dt, ssm_b_re, ssm_b_im, ssm_c_re, ssm_c_im, ssm_d, ssm_w_glu, attn_sinks, w_br_conv, w_br_ssm, w_br_attn, w_out, ln2_g, ln2_b, ffn2_w_gate, ffn2_w_up, ffn2_w_down, ln3_g, ln3_b):
    batch, seq, _ = x.shape
    h = x.reshape(batch * seq, D_MODEL)

    def bf(p):
        return p.astype(BF16)

    ffn1 = (bf(ffn1_w_gate), bf(ffn1_w_up), bf(ffn1_w_down), _vec(ln1_g), _vec(ln1_b))
    ffn2 = (bf(ffn2_w_gate), bf(ffn2_w_up), bf(ffn2_w_down), _vec(ln3_g), _vec(ln3_b))
    mix = _prepare_mixer(dict(
        w_in=w_in, conv_w=conv_w, conv_b=conv_b, ssm_a_re=ssm_a_re, ssm_a_im=ssm_a_im, ssm_log_dt=ssm_log_dt,
        ssm_b_re=ssm_b_re, ssm_b_im=ssm_b_im, ssm_c_re=ssm_c_re, ssm_c_im=ssm_c_im, ssm_d=ssm_d,
        ssm_w_glu=ssm_w_glu, attn_sinks=attn_sinks, w_br_conv=w_br_conv, w_br_ssm=w_br_ssm,
        w_br_attn=w_br_attn, w_out=w_out, ln2_g=ln2_g, ln2_b=ln2_b))

    for l in range(DEPTH):
        h = _ffn_ln(h, l, *ffn1)
        h = _mixer_layer(h, l, batch, mix)
        h = _ffn_ln(h, l, *ffn2)
    return h.reshape(batch, seq, D_MODEL)
```

```python
import functools

import jax
import jax.numpy as jnp
from jax import lax
from jax.experimental import pallas as pl
from jax.experimental.pallas import tpu as pltpu

D_MODEL = 1024
DEPTH = 4
D_CONV = 256
CONV_WIDTH = 3
D_SSM = 256
SSM_GROUP = 16
N_SSM_GROUPS = D_SSM // SSM_GROUP
SSM_STATE = 64
N_STATE = N_SSM_GROUPS * SSM_STATE
N_Q_HEADS = 8
N_KV_HEADS = 2
Q_PER_KV = N_Q_HEADS // N_KV_HEADS
HEAD_DIM = 64
D_ATTN = N_Q_HEADS * HEAD_DIM
D_KV = N_KV_HEADS * HEAD_DIM
WINDOW = 128
N_BRANCH = 3
D_FF = 2816
ALPHA = (2 * DEPTH) ** 0.25
LN_EPS = 1e-5
D_IN = 3 * D_CONV + D_SSM + D_ATTN + 2 * D_KV + N_BRANCH * D_MODEL

OFF_U = 3 * D_CONV
OFF_Q = OFF_U + D_SSM
OFF_K = OFF_Q + D_ATTN
OFF_V = OFF_K + D_KV
OFF_G = OFF_V + D_KV

SUBLANES = 8
LANES = 128
FF_CHUNK = 256
FFN_ROWS = 512
VMEM_LIMIT = 56 * 1024 * 1024

SSM_TILE = 256
SSM_STEPS = SSM_TILE // SUBLANES
N_CHUNKS = N_STATE // LANES
CHUNK_COLS = 2 * LANES
GATE_COLS = 256
MIX_SUBTILES = 2

F32 = jnp.float32
BF16 = jnp.bfloat16


def _dot(a, b):
    return jnp.dot(a, b, preferred_element_type=F32)


def _layer_norm(r, g, b):
    mu = jnp.mean(r, axis=-1, keepdims=True)
    d = r - mu
    var = jnp.mean(d * d, axis=-1, keepdims=True)
    return d * lax.rsqrt(var + LN_EPS) * g + b


def _layer_spec(arr, layer):
    zeros = (0,) * (arr.ndim - 1)
    return pl.BlockSpec((None,) + arr.shape[1:], lambda *_: (layer,) + zeros, pipeline_mode=pl.Buffered(1))


def _params(*sem):
    return pltpu.CompilerParams(dimension_semantics=sem, vmem_limit_bytes=VMEM_LIMIT)


def _ffn_kernel(x_ref, wg_ref, wu_ref, wd_ref, g_ref, b_ref, o_ref):
    for r in range(x_ref.shape[0] // FFN_ROWS):
        rows = slice(r * FFN_ROWS, (r + 1) * FFN_ROWS)
        x = x_ref[rows, :]
        xb = x.astype(BF16)
        acc = jnp.zeros(x.shape, F32)
        for c in range(D_FF // FF_CHUNK):
            cols = slice(c * FF_CHUNK, (c + 1) * FF_CHUNK)
            gate = _dot(xb, wg_ref[:, cols])
            up = _dot(xb, wu_ref[:, cols])
            h = (jax.nn.silu(gate) * up).astype(BF16)
            acc = acc + _dot(h, wd_ref[cols, :])
        o_ref[rows, :] = _layer_norm(ALPHA * x + 0.5 * acc, g_ref[...], b_ref[...])


def _ffn_ln(x, layer, wg, wu, wd, g, b, tm=2 * FFN_ROWS):
    n = x.shape[0]
    row = pl.BlockSpec((tm, D_MODEL), lambda i: (i, 0))
    weights = (wg, wu, wd, g, b)
    return pl.pallas_call(
        _ffn_kernel,
        grid=(n // tm,),
        in_specs=[row] + [_layer_spec(w, layer) for w in weights],
        out_specs=row,
        out_shape=jax.ShapeDtypeStruct((n, D_MODEL), F32),
        compiler_params=_params("parallel"),
        name="ffn_ln",
    )(x, *weights)


def _cmul_add(ar, ai, xr, xi, br, bi):
    return ar * xr - ai * xi + br, ar * xi + ai * xr + bi


def _short_conv(bch, cw_ref, cb_ref, zbuf_ref):
    tt = bch.shape[0]
    z = bch[:, D_CONV:2 * D_CONV] * bch[:, 2 * D_CONV:3 * D_CONV]
    zbuf_ref[SUBLANES:SUBLANES + tt, :] = z
    z1 = zbuf_ref[SUBLANES - 1:SUBLANES - 1 + tt, :]
    z2 = zbuf_ref[SUBLANES - 2:SUBLANES - 2 + tt, :]
    y = cw_ref[0:1, :] * z2 + cw_ref[1:2, :] * z1 + cw_ref[2:3, :] * z + cb_ref[...]
    zbuf_ref[0:SUBLANES, :] = zbuf_ref[tt:tt + SUBLANES, :]
    return (bch[:, 0:D_CONV] * y).astype(BF16)


def _s5_scan(u, bbd_ref, astep_ref, aseg_ref, apow_ref, cbd_ref, st_ref, uh_ref, up_ref, xl_ref, xs_ref, side_work):
    for h in range(D_SSM // LANES):
        uh_ref[h] = u[:, h * LANES:(h + 1) * LANES]
        for r in range(SSM_STEPS):
            up_ref[r * SUBLANES:(r + 1) * SUBLANES, h * LANES:(h + 1) * LANES] = (
                uh_ref[h, pl.ds(r, SUBLANES, stride=SSM_STEPS), :])
    upb = up_ref[...].astype(BF16)
    first_row = lax.broadcasted_iota(jnp.int32, (SUBLANES, LANES), 0) == 0
    yp = None
    for c in range(N_CHUNKS):
        cols = slice(c * CHUNK_COLS, (c + 1) * CHUNK_COLS)
        re = slice(c * CHUNK_COLS, c * CHUNK_COLS + LANES)
        im = slice(c * CHUNK_COLS + LANES, (c + 1) * CHUNK_COLS)
        bu = _dot(upb, bbd_ref[:, cols])
        side_work(c)
        ar, ai = astep_ref[:, re], astep_ref[:, im]
        xr, xi = bu[0:SUBLANES, 0:LANES], bu[0:SUBLANES, LANES:CHUNK_COLS]
        xl_ref[0:SUBLANES, re] = xr
        xl_ref[0:SUBLANES, im] = xi
        for r in range(1, SSM_STEPS):
            rows = slice(r * SUBLANES, (r + 1) * SUBLANES)
            xr, xi = _cmul_add(ar, ai, xr, xi, bu[rows, 0:LANES], bu[rows, LANES:CHUNK_COLS])
            xl_ref[rows, re] = xr
            xl_ref[rows, im] = xi
        sr = jnp.where(first_row, st_ref[:, re], pltpu.roll(xr, 1, 0))
        si = jnp.where(first_row, st_ref[:, im], pltpu.roll(xi, 1, 0))
        for s, shift in enumerate((1, 2, 4)):
            rows = slice(s * SUBLANES, (s + 1) * SUBLANES)
            sr, si = _cmul_add(aseg_ref[rows, re], aseg_ref[rows, im],
                               pltpu.roll(sr, shift, 0), pltpu.roll(si, shift, 0), sr, si)
        rows = slice(3 * SUBLANES, 4 * SUBLANES)
        nr, ni = _cmul_add(aseg_ref[rows, re], aseg_ref[rows, im], sr, si, xr, xi)
        st_ref[:, re] = pltpu.roll(nr, 1, 0)
        st_ref[:, im] = pltpu.roll(ni, 1, 0)
        sr2 = jnp.concatenate([sr, sr], axis=0)
        si2 = jnp.concatenate([si, si], axis=0)
        for k in range(SSM_STEPS // 2):
            rows = slice(2 * k * SUBLANES, 2 * (k + 1) * SUBLANES)
            fr, fi = _cmul_add(apow_ref[rows, re], apow_ref[rows, im], sr2, si2, xl_ref[rows, re], xl_ref[rows, im])
            xs_ref[rows, re] = fr.astype(BF16)
            xs_ref[rows, im] = fi.astype(BF16)
        part = _dot(xs_ref[:, cols], cbd_ref[cols, :])
        yp = part if yp is None else yp + part
    return yp


def _s5_finish(yp, u, d_ref, wglu_ref, yh_ref):
    halves = D_SSM // LANES
    for h in range(halves):
        for r in range(SSM_STEPS):
            yh_ref[h, pl.ds(r, SUBLANES, stride=SSM_STEPS), :] = (
                yp[r * SUBLANES:(r + 1) * SUBLANES, h * LANES:(h + 1) * LANES])
    y = jnp.concatenate([yh_ref[h] for h in range(halves)], axis=-1) + d_ref[...] * u
    y = jax.nn.gelu(y)
    y = y * jax.nn.sigmoid(_dot(y.astype(BF16), wglu_ref[...]))
    return y.astype(BF16)


def _swa_block(q, k_all, v_all, mask, sink_ref, layer):
    dn = (((1,), (1,)), ((), ()))
    outs = []
    for h in range(N_KV_HEADS):
        hs = slice(h * HEAD_DIM, (h + 1) * HEAD_DIM)
        heads = range(h * Q_PER_KV, (h + 1) * Q_PER_KV)
        qh = jnp.concatenate([q[:, j * HEAD_DIM:(j + 1) * HEAD_DIM] for j in heads], axis=0)
        s = lax.dot_general(qh, k_all[:, hs], dn, preferred_element_type=F32)
        probs, denoms = [], []
        for g, j in enumerate(heads):
            sink = sink_ref[layer, j]
            sg = jnp.where(mask, s[g * WINDOW:(g + 1) * WINDOW], -jnp.inf)
            m = jnp.maximum(jnp.max(sg, axis=-1, keepdims=True), sink)
            p = jnp.exp(sg - m)
            denoms.append(jnp.sum(p, axis=-1, keepdims=True) + jnp.exp(sink - m))
            probs.append(p.astype(BF16))
        o = _dot(jnp.concatenate(probs, axis=0), v_all[:, hs])
        outs += [o[g * WINDOW:(g + 1) * WINDOW] / denoms[g] for g in range(Q_PER_KV)]
    return jnp.concatenate(outs, axis=-1).astype(BF16)


def _mixer_kernel(sink_ref, x_ref, win_ref, cw_ref, cb_ref, bbd_ref, astep_ref, aseg_ref, apow_ref, cbd_ref, d_ref,
                  wglu_ref, wc_ref, ws_ref, wa_ref, wo_ref, g_ref, b_ref, o_ref,
                  zbuf_ref, st_ref, uh_ref, up_ref, xl_ref, xs_ref, yh_ref, kv_ref, gates_ref, *, layer):
    first_tile = pl.program_id(1) == 0

    @pl.when(first_tile)
    def _():
        zbuf_ref[0:SUBLANES, :] = jnp.zeros((SUBLANES, D_CONV), F32)
        st_ref[...] = jnp.zeros(st_ref.shape, F32)
        kv_ref[...] = jnp.zeros(kv_ref.shape, BF16)

    row = lax.broadcasted_iota(jnp.int32, (WINDOW, 2 * WINDOW), 0)
    col = lax.broadcasted_iota(jnp.int32, (WINDOW, 2 * WINDOW), 1)
    band = jnp.logical_and(col > row, col - WINDOW <= row)
    first_band = jnp.logical_and(band, jnp.logical_or(col >= WINDOW, jnp.logical_not(first_tile)))

    tt = SSM_TILE
    for sub in range(x_ref.shape[0] // tt):
        tok = slice(sub * tt, (sub + 1) * tt)
        x = x_ref[tok, :]
        xb = x.astype(BF16)

        def proj(lo, hi):
            return _dot(xb, win_ref[:, lo:hi])

        def gate_block(i):
            cols = slice(i * GATE_COLS, (i + 1) * GATE_COLS)
            gates_ref[sub, :, cols] = jax.nn.sigmoid(proj(OFF_G + i * GATE_COLS, OFF_G + (i + 1) * GATE_COLS))

        def side_work(c):
            for i in range(c, N_BRANCH * D_MODEL // GATE_COLS, N_CHUNKS):
                gate_block(i)

        y_conv = _short_conv(proj(0, OFF_U), cw_ref, cb_ref, zbuf_ref)
        u = proj(OFF_U, OFF_Q)
        yp = _s5_scan(u, bbd_ref, astep_ref, aseg_ref, apow_ref, cbd_ref, st_ref, uh_ref, up_ref, xl_ref, xs_ref,
                      side_work)

        q = (proj(OFF_Q, OFF_K) * (HEAD_DIM ** -0.5)).astype(BF16)
        k = jnp.concatenate([kv_ref[0], proj(OFF_K, OFF_V).astype(BF16)], axis=0)
        v = jnp.concatenate([kv_ref[1], proj(OFF_V, OFF_G).astype(BF16)], axis=0)
        kv_ref[0] = k[tt:tt + WINDOW]
        kv_ref[1] = v[tt:tt + WINDOW]
        blocks = []
        for j in range(tt // WINDOW):
            mask = band if (sub or j) else first_band
            blocks.append(_swa_block(q[j * WINDOW:(j + 1) * WINDOW], k[j * WINDOW:(j + 2) * WINDOW],
                                     v[j * WINDOW:(j + 2) * WINDOW], mask, sink_ref, layer))
        y_attn = jnp.concatenate(blocks, axis=0)

        y_ssm = _s5_finish(yp, u, d_ref, wglu_ref, yh_ref)

        t_conv = _dot(y_conv, wc_ref[...])
        t_attn = _dot(y_attn, wa_ref[...])
        t_ssm = _dot(y_ssm, ws_ref[...])
        merged = (gates_ref[sub, :, 0:D_MODEL] * t_conv + gates_ref[sub, :, D_MODEL:2 * D_MODEL] * t_ssm
                  + gates_ref[sub, :, 2 * D_MODEL:3 * D_MODEL] * t_attn)
        mix = _dot(merged.astype(BF16), wo_ref[...])
        o_ref[tok, :] = _layer_norm(ALPHA * x + mix, g_ref[...], b_ref[...])


def _mixer(x, layer, batch, sinks, weights):
    n = x.shape[0]
    tt = SSM_TILE
    tile = MIX_SUBTILES * tt
    per_b = n // batch // tile
    row = pl.BlockSpec((tile, D_MODEL), lambda b, t: (b * per_b + t, 0))
    return pl.pallas_call(
        functools.partial(_mixer_kernel, layer=layer),
        grid=(batch, per_b),
        in_specs=[pl.BlockSpec(memory_space=pltpu.SMEM), row] + [_layer_spec(w, layer) for w in weights],
        out_specs=row,
        out_shape=jax.ShapeDtypeStruct((n, D_MODEL), F32),
        scratch_shapes=[pltpu.VMEM((tt + SUBLANES, D_CONV), F32),
                        pltpu.VMEM((SUBLANES, 2 * N_STATE), F32),
                        pltpu.VMEM((D_SSM // LANES, tt, LANES), F32),
                        pltpu.VMEM((tt, D_SSM), F32),
                        pltpu.VMEM((tt, 2 * N_STATE), F32),
                        pltpu.VMEM((tt, 2 * N_STATE), BF16),
                        pltpu.VMEM((D_SSM // LANES, tt, LANES), F32),
                        pltpu.VMEM((2, WINDOW, D_KV), BF16),
                        pltpu.VMEM((MIX_SUBTILES, tt, N_BRANCH * D_MODEL), F32)],
        compiler_params=_params("arbitrary", "arbitrary"),
        name="mixer",
    )(sinks, x, *weights)


def _chunked(re, im, axis):
    def split(a):
        return a.reshape(a.shape[:axis] + (N_CHUNKS, 1, LANES) + a.shape[axis + 1:])

    both = jnp.concatenate([split(re), split(im)], axis=axis + 1)
    return both.reshape(re.shape[:axis] + (2 * N_STATE,) + re.shape[axis + 1:])


def _ssm_tables(a_re, a_im, log_dt, b_re, b_im, c_re, c_im):
    dt = jnp.exp(log_dt)[..., None]
    mag = jnp.exp(a_re * dt)
    ang = a_im * dt
    abar_re = mag * jnp.cos(ang)
    abar_im = mag * jnp.sin(ang)
    nr = abar_re - 1.0
    ni = abar_im
    den = a_re * a_re + a_im * a_im
    coef_re = (nr * a_re + ni * a_im) / den
    coef_im = (ni * a_re - nr * a_im) / den
    bbar_re = coef_re[..., None] * b_re - coef_im[..., None] * b_im
    bbar_im = coef_re[..., None] * b_im + coef_im[..., None] * b_re

    eye = jnp.eye(N_SSM_GROUPS, dtype=F32)

    def in_blockdiag(w):
        return jnp.einsum('lgpc,gh->lgchp', w, eye).reshape(DEPTH, D_SSM, N_STATE)

    def out_blockdiag(w):
        return jnp.einsum('lgcp,gh->lgphc', w, eye).reshape(DEPTH, N_STATE, D_SSM)

    bbd = _chunked(in_blockdiag(bbar_re), in_blockdiag(bbar_im), axis=2).astype(BF16)
    cbd = _chunked(out_blockdiag(c_re), -out_blockdiag(c_im), axis=1).astype(BF16)

    pr, pi = abar_re.reshape(DEPTH, 1, N_STATE), abar_im.reshape(DEPTH, 1, N_STATE)
    while pr.shape[1] < SSM_STEPS:
        tr, ti = pr[:, -1:], pi[:, -1:]
        pr, pi = (jnp.concatenate([pr, pr * tr - pi * ti], axis=1),
                  jnp.concatenate([pi, pr * ti + pi * tr], axis=1))

    def bcast(a, rows):
        return jnp.broadcast_to(a, (DEPTH, rows, N_STATE))

    astep = _chunked(bcast(pr[:, 0:1], SUBLANES), bcast(pi[:, 0:1], SUBLANES), axis=2)
    apow = _chunked(jnp.repeat(pr, SUBLANES, axis=1), jnp.repeat(pi, SUBLANES, axis=1), axis=2)

    sublane = jnp.arange(SUBLANES)[None, :, None]
    qr, qi = pr[:, -1:], pi[:, -1:]
    seg_re, seg_im = [], []
    for shift in (1, 2, 4):
        seg_re.append(jnp.where(sublane >= shift, bcast(qr, SUBLANES), 0.0))
        seg_im.append(jnp.where(sublane >= shift, bcast(qi, SUBLANES), 0.0))
        qr, qi = qr * qr - qi * qi, 2.0 * qr * qi
    seg_re.append(bcast(pr[:, -1:], SUBLANES))
    seg_im.append(bcast(pi[:, -1:], SUBLANES))
    aseg = _chunked(jnp.concatenate(seg_re, axis=1), jnp.concatenate(seg_im, axis=1), axis=2)
    return bbd, astep, aseg, apow, cbd


def _vec(p):
    return p.reshape(DEPTH, 1, -1)


def _prepare_mixer(p):
    bf = lambda a: a.astype(BF16)
    bbd, astep, aseg, apow, cbd = _ssm_tables(p['ssm_a_re'], p['ssm_a_im'], p['ssm_log_dt'], p['ssm_b_re'],
                                              p['ssm_b_im'], p['ssm_c_re'], p['ssm_c_im'])
    weights = (bf(p['w_in']), p['conv_w'], _vec(p['conv_b']), bbd, astep, aseg, apow, cbd, _vec(p['ssm_d']),
               bf(p['ssm_w_glu']), bf(p['w_br_conv']), bf(p['w_br_ssm']), bf(p['w_br_attn']), bf(p['w_out']),
               _vec(p['ln2_g']), _vec(p['ln2_b']))
    return p['attn_sinks'], weights


def _mixer_layer(h, layer, batch, mix):
    sinks, weights = mix
    return _mixer(h, layer, batch, sinks, weights)


def kernel(x, ffn1_w_gate, ffn1_w_up, ffn1_w_down, ln1_g, ln1_b, w_in, conv_w, conv_b, ssm_a_re, ssm_a_im, ssm_log_dt, ssm_b_re, ssm_b_im, ssm_c_re, ssm_c_im, ssm_d, ssm_w_glu, attn_sinks, w_br_conv, w_br_ssm, w_br_attn, w_out, ln2_g, ln2_b, ffn2_w_gate, ffn2_w_up, ffn2_w_down, ln3_g, ln3_b):
    batch, seq, _ = x.shape
    h = x.reshape(batch * seq, D_MODEL)

    def bf(p):
        return p.astype(BF16)

    ffn1 = (bf(ffn1_w_gate), bf(ffn1_w_up), bf(ffn1_w_down), _vec(ln1_g), _vec(ln1_b))
    ffn2 = (bf(ffn2_w_gate), bf(ffn2_w_up), bf(ffn2_w_down), _vec(ln3_g), _vec(ln3_b))
    mix = _prepare_mixer(dict(
        w_in=w_in, conv_w=conv_w, conv_b=conv_b, ssm_a_re=ssm_a_re, ssm_a_im=ssm_a_im, ssm_log_dt=ssm_log_dt,
        ssm_b_re=ssm_b_re, ssm_b_im=ssm_b_im, ssm_c_re=ssm_c_re, ssm_c_im=ssm_c_im, ssm_d=ssm_d,
        ssm_w_glu=ssm_w_glu, attn_sinks=attn_sinks, w_br_conv=w_br_conv, w_br_ssm=w_br_ssm,
        w_br_attn=w_br_attn, w_out=w_out, ln2_g=ln2_g, ln2_b=ln2_b))

    for l in range(DEPTH):
        h = _ffn_ln(h, l, *ffn1)
        h = _mixer_layer(h, l, batch, mix)
        h = _ffn_ln(h, l, *ffn2)
    return h.reshape(batch, seq, D_MODEL)
```

```python
import functools

import jax
import jax.numpy as jnp
from jax import lax
from jax.experimental import pallas as pl
from jax.experimental.pallas import tpu as pltpu

D_MODEL = 1024
DEPTH = 4
D_CONV = 256
CONV_WIDTH = 3
D_SSM = 256
SSM_GROUP = 16
N_SSM_GROUPS = D_SSM // SSM_GROUP
SSM_STATE = 64
N_STATE = N_SSM_GROUPS * SSM_STATE
N_Q_HEADS = 8
N_KV_HEADS = 2
Q_PER_KV = N_Q_HEADS // N_KV_HEADS
HEAD_DIM = 64
D_ATTN = N_Q_HEADS * HEAD_DIM
D_KV = N_KV_HEADS * HEAD_DIM
WINDOW = 128
N_BRANCH = 3
D_FF = 2816
ALPHA = (2 * DEPTH) ** 0.25
LN_EPS = 1e-5
D_IN = 3 * D_CONV + D_SSM + D_ATTN + 2 * D_KV + N_BRANCH * D_MODEL

OFF_U = 3 * D_CONV
OFF_Q = OFF_U + D_SSM
OFF_K = OFF_Q + D_ATTN
OFF_V = OFF_K + D_KV
OFF_G = OFF_V + D_KV

SUBLANES = 8
LANES = 128
BF16_ROWS = 16
FF_CHUNK = 256
FFN_ROWS = 512
VMEM_LIMIT = 56 * 1024 * 1024

SSM_TILE = 256
SSM_STEPS = SSM_TILE // SUBLANES
N_CHUNKS = N_STATE // LANES
CHUNK_COLS = 2 * LANES
GATE_COLS = 256
MIX_SUBTILES = 2
GATES_AHEAD_OF_SCAN = 2
GATES_PER_CHUNK = 1

F32 = jnp.float32
BF16 = jnp.bfloat16


def _dot(a, b):
    return jnp.dot(a, b, preferred_element_type=F32)


def _layer_norm(r, g, b):
    mu = jnp.mean(r, axis=-1, keepdims=True)
    d = r - mu
    var = jnp.mean(d * d, axis=-1, keepdims=True)
    return d * lax.rsqrt(var + LN_EPS) * g + b


def _resident_spec(arr, layer=None):
    if layer is None:
        return pl.BlockSpec(arr.shape, lambda *_: (0,) * arr.ndim, pipeline_mode=pl.Buffered(1))
    zeros = (0,) * (arr.ndim - 1)
    return pl.BlockSpec((None,) + arr.shape[1:], lambda *_: (layer,) + zeros, pipeline_mode=pl.Buffered(1))


def _cargo_specs(cargo, layer, steps, step_of):
    in_specs, out_specs, out_shapes = [], [], []
    for arr, _ in cargo:
        _, rows, cols = arr.shape
        blocks = steps if rows % (steps * BF16_ROWS) == 0 else steps // 2
        assert rows % (blocks * BF16_ROWS) == 0 and steps % blocks == 0, (arr.shape, steps)
        per = steps // blocks
        in_specs.append(pl.BlockSpec((None, rows // blocks, cols),
                                     lambda *ids, per=per: (layer, step_of(*ids) // per, 0)))
        out_specs.append(pl.BlockSpec((rows // blocks, cols), lambda *ids, per=per: (step_of(*ids) // per, 0)))
        out_shapes.append(jax.ShapeDtypeStruct((rows, cols), BF16))
    return in_specs, out_specs, out_shapes


def _narrow_cargo(in_refs, out_refs, scales):
    for src, dst, scale in zip(in_refs, out_refs, scales):
        w = src[...]
        dst[...] = (w if scale == 1.0 else w * scale).astype(BF16)


def _params(*sem):
    return pltpu.CompilerParams(dimension_semantics=sem, vmem_limit_bytes=VMEM_LIMIT)


def _ffn_kernel(x_ref, wg_ref, wu_ref, wd_ref, g_ref, b_ref, *rest, cargo_scales):
    n_cargo = len(cargo_scales)
    o_ref = rest[n_cargo]
    _narrow_cargo(rest[:n_cargo], rest[n_cargo + 1:], cargo_scales)
    xb_all = x_ref[...].astype(BF16)
    for r in range(x_ref.shape[0] // FFN_ROWS):
        rows = slice(r * FFN_ROWS, (r + 1) * FFN_ROWS)
        x = x_ref[rows, :]
        xb = xb_all[rows, :]
        acc = jnp.zeros(x.shape, F32)
        for c in range(D_FF // FF_CHUNK):
            cols = slice(c * FF_CHUNK, (c + 1) * FF_CHUNK)
            gate = _dot(xb, wg_ref[:, cols])
            up = _dot(xb, wu_ref[:, cols])
            h = (jax.nn.silu(gate) * up).astype(BF16)
            acc = acc + _dot(h, wd_ref[cols, :])
        o_ref[rows, :] = _layer_norm(ALPHA * x + acc, g_ref[...], b_ref[...])


def _ffn_ln(x, layer, wg, wu, wd, g, b, cargo=(), cargo_layer=0, tm=2 * FFN_ROWS):
    n = x.shape[0]
    steps = n // tm
    row = pl.BlockSpec((tm, D_MODEL), lambda i: (i, 0))
    c_in, c_out, c_shapes = _cargo_specs(cargo, cargo_layer, steps, lambda i: i)
    out = pl.pallas_call(
        functools.partial(_ffn_kernel, cargo_scales=tuple(sc for _, sc in cargo)),
        grid=(steps,),
        in_specs=[row, _resident_spec(wg), _resident_spec(wu), _resident_spec(wd), _resident_spec(g, layer),
                  _resident_spec(b, layer)] + c_in,
        out_specs=[row] + c_out,
        out_shape=[jax.ShapeDtypeStruct((n, D_MODEL), F32)] + c_shapes,
        compiler_params=_params("parallel"),
        name="ffn_ln",
    )(x, wg, wu, wd, g, b, *[a for a, _ in cargo])
    return out[0], out[1:]


def _cmul_add(ar, ai, xr, xi, br, bi):
    return ar * xr - ai * xi + br, ar * xi + ai * xr + bi


def _short_conv(bch, cw_ref, cb_ref, zbuf_ref):
    tt = bch.shape[0]
    z = bch[:, D_CONV:2 * D_CONV] * bch[:, 2 * D_CONV:3 * D_CONV]
    zbuf_ref[SUBLANES:SUBLANES + tt, :] = z
    z1 = zbuf_ref[SUBLANES - 1:SUBLANES - 1 + tt, :]
    z2 = zbuf_ref[SUBLANES - 2:SUBLANES - 2 + tt, :]
    y = cw_ref[0:1, :] * z2 + cw_ref[1:2, :] * z1 + cw_ref[2:3, :] * z + cb_ref[...]
    zbuf_ref[0:SUBLANES, :] = zbuf_ref[tt:tt + SUBLANES, :]
    return (bch[:, 0:D_CONV] * y).astype(BF16)


def _s5_scan(u, bbd_ref, astep_ref, aseg_ref, apow_ref, cbd_ref, st_ref, uh_ref, up_ref, xl_ref, xs_ref, side_work):
    for h in range(D_SSM // LANES):
        uh_ref[h] = u[:, h * LANES:(h + 1) * LANES]
        for r in range(SSM_STEPS):
            up_ref[r * SUBLANES:(r + 1) * SUBLANES, h * LANES:(h + 1) * LANES] = (
                uh_ref[h, pl.ds(r, SUBLANES, stride=SSM_STEPS), :])
    upb = up_ref[...].astype(BF16)
    first_row = lax.broadcasted_iota(jnp.int32, (SUBLANES, LANES), 0) == 0
    yp = None
    for c in range(N_CHUNKS):
        cols = slice(c * CHUNK_COLS, (c + 1) * CHUNK_COLS)
        re = slice(c * CHUNK_COLS, c * CHUNK_COLS + LANES)
        im = slice(c * CHUNK_COLS + LANES, (c + 1) * CHUNK_COLS)
        bu = _dot(upb, bbd_ref[:, cols])
        side_work(c)
        ar, ai = astep_ref[:, re], astep_ref[:, im]
        xr, xi = bu[0:SUBLANES, 0:LANES], bu[0:SUBLANES, LANES:CHUNK_COLS]
        xl_ref[0:SUBLANES, re] = xr
        xl_ref[0:SUBLANES, im] = xi
        for r in range(1, SSM_STEPS):
            rows = slice(r * SUBLANES, (r + 1) * SUBLANES)
            xr, xi = _cmul_add(ar, ai, xr, xi, bu[rows, 0:LANES], bu[rows, LANES:CHUNK_COLS])
            xl_ref[rows, re] = xr
            xl_ref[rows, im] = xi
        sr = jnp.where(first_row, st_ref[:, re], pltpu.roll(xr, 1, 0))
        si = jnp.where(first_row, st_ref[:, im], pltpu.roll(xi, 1, 0))
        for s, shift in enumerate((1, 2, 4)):
            rows = slice(s * SUBLANES, (s + 1) * SUBLANES)
            sr, si = _cmul_add(aseg_ref[rows, re], aseg_ref[rows, im],
                               pltpu.roll(sr, shift, 0), pltpu.roll(si, shift, 0), sr, si)
        rows = slice(3 * SUBLANES, 4 * SUBLANES)
        nr, ni = _cmul_add(aseg_ref[rows, re], aseg_ref[rows, im], sr, si, xr, xi)
        st_ref[:, re] = pltpu.roll(nr, 1, 0)
        st_ref[:, im] = pltpu.roll(ni, 1, 0)
        sr2 = jnp.concatenate([sr, sr], axis=0)
        si2 = jnp.concatenate([si, si], axis=0)
        for k in range(SSM_STEPS // 2):
            rows = slice(2 * k * SUBLANES, 2 * (k + 1) * SUBLANES)
            fr, fi = _cmul_add(apow_ref[rows, re], apow_ref[rows, im], sr2, si2, xl_ref[rows, re], xl_ref[rows, im])
            xs_ref[rows, re] = fr.astype(BF16)
            xs_ref[rows, im] = fi.astype(BF16)
        part = _dot(xs_ref[:, cols], cbd_ref[cols, :])
        yp = part if yp is None else yp + part
    return yp


def _s5_finish(yp, u, d_ref, wglu_ref, yh_ref):
    halves = D_SSM // LANES
    for h in range(halves):
        for r in range(SSM_STEPS):
            yh_ref[h, pl.ds(r, SUBLANES, stride=SSM_STEPS), :] = (
                yp[r * SUBLANES:(r + 1) * SUBLANES, h * LANES:(h + 1) * LANES])
    y = jnp.concatenate([yh_ref[h] for h in range(halves)], axis=-1) + d_ref[...] * u
    y = jax.nn.gelu(y)
    y = y * jax.nn.sigmoid(_dot(y.astype(BF16), wglu_ref[...]))
    return y.astype(BF16)


def _swa_block(q, k_all, v_all, mask, sink_ref, layer, side_work):
    dn = (((1,), (1,)), ((), ()))
    outs = []
    for h in range(N_KV_HEADS):
        hs = slice(h * HEAD_DIM, (h + 1) * HEAD_DIM)
        heads = range(h * Q_PER_KV, (h + 1) * Q_PER_KV)
        qh = jnp.concatenate([q[:, j * HEAD_DIM:(j + 1) * HEAD_DIM] for j in heads], axis=0)
        s = lax.dot_general(qh, k_all[:, hs], dn, preferred_element_type=F32)
        side_work()
        probs, denoms = [], []
        for g, j in enumerate(heads):
            sink = sink_ref[layer, j]
            sg = jnp.where(mask, s[g * WINDOW:(g + 1) * WINDOW], -jnp.inf)
            m = jnp.maximum(jnp.max(sg, axis=-1, keepdims=True), sink)
            p = jnp.exp(sg - m)
            denoms.append(jnp.sum(p, axis=-1, keepdims=True) + jnp.exp(sink - m))
            probs.append(p.astype(BF16))
        o = _dot(jnp.concatenate(probs, axis=0), v_all[:, hs])
        outs += [o[g * WINDOW:(g + 1) * WINDOW] / denoms[g] for g in range(Q_PER_KV)]
    return jnp.concatenate(outs, axis=-1).astype(BF16)


def _mixer_kernel(sink_ref, x_ref, win_ref, wglu_ref, wc_ref, ws_ref, wa_ref, wo_ref, cw_ref, cb_ref, bbd_ref,
                  astep_ref, aseg_ref, apow_ref, cbd_ref, d_ref, g_ref, b_ref, *rest, layer, cargo_scales):
    n_cargo = len(cargo_scales)
    o_ref = rest[n_cargo]
    zbuf_ref, st_ref, uh_ref, up_ref, xl_ref, xs_ref, yh_ref, kv_ref, gates_ref = rest[2 * n_cargo + 1:]
    _narrow_cargo(rest[:n_cargo], rest[n_cargo + 1:2 * n_cargo + 1], cargo_scales)
    first_tile = pl.program_id(1) == 0

    @pl.when(first_tile)
    def _():
        zbuf_ref[0:SUBLANES, :] = jnp.zeros((SUBLANES, D_CONV), F32)
        st_ref[...] = jnp.zeros(st_ref.shape, F32)
        kv_ref[...] = jnp.zeros(kv_ref.shape, BF16)

    row = lax.broadcasted_iota(jnp.int32, (WINDOW, 2 * WINDOW), 0)
    col = lax.broadcasted_iota(jnp.int32, (WINDOW, 2 * WINDOW), 1)
    band = jnp.logical_and(col > row, col - WINDOW <= row)
    first_band = jnp.logical_and(band, jnp.logical_or(col >= WINDOW, jnp.logical_not(first_tile)))

    tt = SSM_TILE
    for sub in range(x_ref.shape[0] // tt):
        tok = slice(sub * tt, (sub + 1) * tt)
        x = x_ref[tok, :]
        xb = x.astype(BF16)

        def proj(lo, hi):
            return _dot(xb, win_ref[:, lo:hi])

        def gate_block(i):
            cols = slice(i * GATE_COLS, (i + 1) * GATE_COLS)
            gates_ref[sub, :, cols] = jax.nn.sigmoid(proj(OFF_G + i * GATE_COLS, OFF_G + (i + 1) * GATE_COLS))

        pending = list(range(N_BRANCH * D_MODEL // GATE_COLS))

        def gates(n):
            for _ in range(min(n, len(pending))):
                gate_block(pending.pop(0))

        u = proj(OFF_U, OFF_Q)
        y_conv = _short_conv(proj(0, OFF_U), cw_ref, cb_ref, zbuf_ref)
        gates(GATES_AHEAD_OF_SCAN)
        yp = _s5_scan(u, bbd_ref, astep_ref, aseg_ref, apow_ref, cbd_ref, st_ref, uh_ref, up_ref, xl_ref, xs_ref,
                      lambda c: gates(GATES_PER_CHUNK))

        q = (proj(OFF_Q, OFF_K) * (HEAD_DIM ** -0.5)).astype(BF16)
        k = jnp.concatenate([kv_ref[0], proj(OFF_K, OFF_V).astype(BF16)], axis=0)
        v = jnp.concatenate([kv_ref[1], proj(OFF_V, OFF_G).astype(BF16)], axis=0)
        kv_ref[0] = k[tt:tt + WINDOW]
        kv_ref[1] = v[tt:tt + WINDOW]
        blocks = []
        for j in range(tt // WINDOW):
            mask = band if (sub or j) else first_band
            blocks.append(_swa_block(q[j * WINDOW:(j + 1) * WINDOW], k[j * WINDOW:(j + 2) * WINDOW],
                                     v[j * WINDOW:(j + 2) * WINDOW], mask, sink_ref, layer, lambda: gates(1)))
        y_attn = jnp.concatenate(blocks, axis=0)
        gates(len(pending))

        y_ssm = _s5_finish(yp, u, d_ref, wglu_ref, yh_ref)

        t_conv = _dot(y_conv, wc_ref[...])
        t_attn = _dot(y_attn, wa_ref[...])
        t_ssm = _dot(y_ssm, ws_ref[...])
        merged = (gates_ref[sub, :, 0:D_MODEL] * t_conv + gates_ref[sub, :, D_MODEL:2 * D_MODEL] * t_ssm
                  + gates_ref[sub, :, 2 * D_MODEL:3 * D_MODEL] * t_attn)
        mix = _dot(merged.astype(BF16), wo_ref[...])
        o_ref[tok, :] = _layer_norm(ALPHA * x + mix, g_ref[...], b_ref[...])


def _mixer(x, layer, batch, sinks, narrow_w, tables, cargo=()):
    n = x.shape[0]
    tt = SSM_TILE
    tile = MIX_SUBTILES * tt
    per_b = n // batch // tile
    row = pl.BlockSpec((tile, D_MODEL), lambda b, t: (b * per_b + t, 0))
    c_in, c_out, c_shapes = _cargo_specs(cargo, layer, batch * per_b, lambda b, t: b * per_b + t)
    out = pl.pallas_call(
        functools.partial(_mixer_kernel, layer=layer, cargo_scales=tuple(sc for _, sc in cargo)),
        grid=(batch, per_b),
        in_specs=[pl.BlockSpec(memory_space=pltpu.SMEM), row] + [_resident_spec(w) for w in narrow_w]
        + [_resident_spec(w, layer) for w in tables] + c_in,
        out_specs=[row] + c_out,
        out_shape=[jax.ShapeDtypeStruct((n, D_MODEL), F32)] + c_shapes,
        scratch_shapes=[pltpu.VMEM((tt + SUBLANES, D_CONV), F32),
                        pltpu.VMEM((SUBLANES, 2 * N_STATE), F32),
                        pltpu.VMEM((D_SSM // LANES, tt, LANES), F32),
                        pltpu.VMEM((tt, D_SSM), F32),
                        pltpu.VMEM((tt, 2 * N_STATE), F32),
                        pltpu.VMEM((tt, 2 * N_STATE), BF16),
                        pltpu.VMEM((D_SSM // LANES, tt, LANES), F32),
                        pltpu.VMEM((2, WINDOW, D_KV), BF16),
                        pltpu.VMEM((MIX_SUBTILES, tt, N_BRANCH * D_MODEL), F32)],
        compiler_params=_params("arbitrary", "arbitrary"),
        name="mixer",
    )(sinks, x, *narrow_w, *tables, *[a for a, _ in cargo])
    return out[0], out[1:]


def _chunked(re, im, axis):
    def split(a):
        return a.reshape(a.shape[:axis] + (N_CHUNKS, 1, LANES) + a.shape[axis + 1:])

    both = jnp.concatenate([split(re), split(im)], axis=axis + 1)
    return both.reshape(re.shape[:axis] + (2 * N_STATE,) + re.shape[axis + 1:])


def _ssm_tables(a_re, a_im, log_dt, b_re, b_im, c_re, c_im):
    dt = jnp.exp(log_dt)[..., None]
    mag = jnp.exp(a_re * dt)
    ang = a_im * dt
    abar_re = mag * jnp.cos(ang)
    abar_im = mag * jnp.sin(ang)
    nr = abar_re - 1.0
    ni = abar_im
    den = a_re * a_re + a_im * a_im
    coef_re = (nr * a_re + ni * a_im) / den
    coef_im = (ni * a_re - nr * a_im) / den
    bbar_re = coef_re[..., None] * b_re - coef_im[..., None] * b_im
    bbar_im = coef_re[..., None] * b_im + coef_im[..., None] * b_re

    eye = jnp.eye(N_SSM_GROUPS, dtype=F32)

    def in_blockdiag(w):
        return jnp.einsum('lgpc,gh->lgchp', w, eye).reshape(DEPTH, D_SSM, N_STATE)

    def out_blockdiag(w):
        return jnp.einsum('lgcp,gh->lgphc', w, eye).reshape(DEPTH, N_STATE, D_SSM)

    bbd = _chunked(in_blockdiag(bbar_re), in_blockdiag(bbar_im), axis=2).astype(BF16)
    cbd = _chunked(out_blockdiag(c_re), -out_blockdiag(c_im), axis=1).astype(BF16)

    pr, pi = abar_re.reshape(DEPTH, 1, N_STATE), abar_im.reshape(DEPTH, 1, N_STATE)
    while pr.shape[1] < SSM_STEPS:
        tr, ti = pr[:, -1:], pi[:, -1:]
        pr, pi = (jnp.concatenate([pr, pr * tr - pi * ti], axis=1),
                  jnp.concatenate([pi, pr * ti + pi * tr], axis=1))

    def bcast(a, rows):
        return jnp.broadcast_to(a, (DEPTH, rows, N_STATE))

    astep = _chunked(bcast(pr[:, 0:1], SUBLANES), bcast(pi[:, 0:1], SUBLANES), axis=2)
    apow = _chunked(jnp.repeat(pr, SUBLANES, axis=1), jnp.repeat(pi, SUBLANES, axis=1), axis=2)

    sublane = jnp.arange(SUBLANES)[None, :, None]
    qr, qi = pr[:, -1:], pi[:, -1:]
    seg_re, seg_im = [], []
    for shift in (1, 2, 4):
        seg_re.append(jnp.where(sublane >= shift, bcast(qr, SUBLANES), 0.0))
        seg_im.append(jnp.where(sublane >= shift, bcast(qi, SUBLANES), 0.0))
        qr, qi = qr * qr - qi * qi, 2.0 * qr * qi
    seg_re.append(bcast(pr[:, -1:], SUBLANES))
    seg_im.append(bcast(pi[:, -1:], SUBLANES))
    aseg = _chunked(jnp.concatenate(seg_re, axis=1), jnp.concatenate(seg_im, axis=1), axis=2)
    return bbd, astep, aseg, apow, cbd


def _vec(p):
    return p.reshape(DEPTH, 1, -1)


MIXER_NARROW = ('w_in', 'ssm_w_glu', 'w_br_conv', 'w_br_ssm', 'w_br_attn', 'w_out')


def _mixer_tables(p):
    bbd, astep, aseg, apow, cbd = _ssm_tables(p['ssm_a_re'], p['ssm_a_im'], p['ssm_log_dt'], p['ssm_b_re'],
                                              p['ssm_b_im'], p['ssm_c_re'], p['ssm_c_im'])
    return (p['conv_w'], _vec(p['conv_b']), bbd, astep, aseg, apow, cbd, _vec(p['ssm_d']),
            _vec(p['ln2_g']), _vec(p['ln2_b']))


def kernel(x, ffn1_w_gate, ffn1_w_up, ffn1_w_down, ln1_g, ln1_b, w_in, conv_w, conv_b, ssm_a_re, ssm_a_im, ssm_log_dt, ssm_b_re, ssm_b_im, ssm_c_re, ssm_c_im, ssm_d, ssm_w_glu, attn_sinks, w_br_conv, w_br_ssm, w_br_attn, w_out, ln2_g, ln2_b, ffn2_w_gate, ffn2_w_up, ffn2_w_down, ln3_g, ln3_b):
    batch, seq, _ = x.shape
    h = x.reshape(batch * seq, D_MODEL)
    p = dict(w_in=w_in, conv_w=conv_w, conv_b=conv_b, ssm_a_re=ssm_a_re, ssm_a_im=ssm_a_im, ssm_log_dt=ssm_log_dt,
             ssm_b_re=ssm_b_re, ssm_b_im=ssm_b_im, ssm_c_re=ssm_c_re, ssm_c_im=ssm_c_im, ssm_d=ssm_d,
             ssm_w_glu=ssm_w_glu, w_br_conv=w_br_conv, w_br_ssm=w_br_ssm, w_br_attn=w_br_attn, w_out=w_out,
             ln2_g=ln2_g, ln2_b=ln2_b)
    tables = _mixer_tables(p)
    mixer_cargo = [(p[name], 1.0) for name in MIXER_NARROW]
    ffn1_cargo = [(ffn1_w_gate, 1.0), (ffn1_w_up, 1.0), (ffn1_w_down, 0.5)]
    ffn2_cargo = [(ffn2_w_gate, 1.0), (ffn2_w_up, 1.0), (ffn2_w_down, 0.5)]
    ffn_w = (ffn1_w_gate[0].astype(BF16), ffn1_w_up[0].astype(BF16), (0.5 * ffn1_w_down[0]).astype(BF16))
    ln1, ln3 = (_vec(ln1_g), _vec(ln1_b)), (_vec(ln3_g), _vec(ln3_b))
    for l in range(DEPTH):
        h, mix_w = _ffn_ln(h, l, *ffn_w, *ln1, cargo=mixer_cargo, cargo_layer=l)
        h, ffn_w = _mixer(h, l, batch, attn_sinks, mix_w, tables, cargo=ffn2_cargo)
        last = l == DEPTH - 1
        h, ffn_w = _ffn_ln(h, l, *ffn_w, *ln3, cargo=() if last else ffn1_cargo, cargo_layer=l + 1)
    return h.reshape(batch, seq, D_MODEL)
```

```python
import functools

import jax
import jax.numpy as jnp
from jax import lax
from jax.experimental import pallas as pl
from jax.experimental.pallas import tpu as pltpu

D_MODEL = 1024
DEPTH = 4
D_CONV = 256
CONV_WIDTH = 3
D_SSM = 256
SSM_GROUP = 16
N_SSM_GROUPS = D_SSM // SSM_GROUP
SSM_STATE = 64
N_STATE = N_SSM_GROUPS * SSM_STATE
N_Q_HEADS = 8
N_KV_HEADS = 2
Q_PER_KV = N_Q_HEADS // N_KV_HEADS
HEAD_DIM = 64
D_ATTN = N_Q_HEADS * HEAD_DIM
D_KV = N_KV_HEADS * HEAD_DIM
WINDOW = 128
N_BRANCH = 3
D_FF = 2816
ALPHA = (2 * DEPTH) ** 0.25
LN_EPS = 1e-5
D_IN = 3 * D_CONV + D_SSM + D_ATTN + 2 * D_KV + N_BRANCH * D_MODEL

OFF_U = 3 * D_CONV
OFF_Q = OFF_U + D_SSM
OFF_K = OFF_Q + D_ATTN
OFF_V = OFF_K + D_KV
OFF_G = OFF_V + D_KV

SUBLANES = 8
LANES = 128
BF16_ROWS = 16
FF_CHUNK = 256
FFN_ROWS = 512
VMEM_LIMIT = 56 * 1024 * 1024

SSM_TILE = 256
SSM_STEPS = SSM_TILE // SUBLANES
N_CHUNKS = N_STATE // LANES
CHUNK_COLS = 2 * LANES
GATE_COLS = 256
MIX_SUBTILES = 2
GATES_AHEAD_OF_SCAN = 2
GATES_PER_CHUNK = 1

F32 = jnp.float32
BF16 = jnp.bfloat16


def _dot(a, b):
    return jnp.dot(a, b, preferred_element_type=F32)


def _layer_norm(r, g, b):
    mu = jnp.mean(r, axis=-1, keepdims=True)
    d = r - mu
    var = jnp.mean(d * d, axis=-1, keepdims=True)
    return d * lax.rsqrt(var + LN_EPS) * g + b


def _resident_spec(arr, layer=None):
    if layer is None:
        return pl.BlockSpec(arr.shape, lambda *_: (0,) * arr.ndim, pipeline_mode=pl.Buffered(1))
    zeros = (0,) * (arr.ndim - 1)
    return pl.BlockSpec((None,) + arr.shape[1:], lambda *_: (layer,) + zeros, pipeline_mode=pl.Buffered(1))


def _cargo_specs(cargo, layer, steps, step_of):
    in_specs, out_specs, out_shapes = [], [], []
    for arr, _ in cargo:
        _, rows, cols = arr.shape
        blocks = steps if rows % (steps * BF16_ROWS) == 0 else steps // 2
        assert rows % (blocks * BF16_ROWS) == 0 and steps % blocks == 0, (arr.shape, steps)
        per = steps // blocks
        in_specs.append(pl.BlockSpec((None, rows // blocks, cols),
                                     lambda *ids, per=per: (layer, step_of(*ids) // per, 0)))
        out_specs.append(pl.BlockSpec((rows // blocks, cols), lambda *ids, per=per: (step_of(*ids) // per, 0)))
        out_shapes.append(jax.ShapeDtypeStruct((rows, cols), BF16))
    return in_specs, out_specs, out_shapes


def _narrow_cargo(in_refs, out_refs, scales):
    for src, dst, scale in zip(in_refs, out_refs, scales):
        w = src[...]
        dst[...] = (w if scale == 1.0 else w * scale).astype(BF16)


def _params(*sem):
    return pltpu.CompilerParams(dimension_semantics=sem, vmem_limit_bytes=VMEM_LIMIT)


def _ffn_kernel(x_ref, wg_ref, wu_ref, wd_ref, g_ref, b_ref, *rest, cargo_scales):
    n_cargo = len(cargo_scales)
    o_ref = rest[n_cargo]
    h_ref = rest[-1]
    n_chunks = D_FF // FF_CHUNK
    xb_all = x_ref[...].astype(BF16)
    for r in range(x_ref.shape[0] // FFN_ROWS):
        rows = slice(r * FFN_ROWS, (r + 1) * FFN_ROWS)
        xb = xb_all[rows, :]
        for c in range(n_chunks):
            cols = slice(c * FF_CHUNK, (c + 1) * FF_CHUNK)
            gate = _dot(xb, wg_ref[:, cols])
            up = _dot(xb, wu_ref[:, cols])
            h_ref[r, :, cols] = (jax.nn.silu(gate) * up).astype(BF16)
            if r == 0 and c == n_chunks // 2:
                _narrow_cargo(rest[:n_cargo], rest[n_cargo + 1:2 * n_cargo + 1], cargo_scales)
        acc = _dot(h_ref[r], wd_ref[...])
        o_ref[rows, :] = _layer_norm(ALPHA * x_ref[rows, :] + acc, g_ref[...], b_ref[...])


def _ffn_ln(x, layer, wg, wu, wd, g, b, cargo=(), cargo_layer=0, tm=2 * FFN_ROWS):
    n = x.shape[0]
    steps = n // tm
    row = pl.BlockSpec((tm, D_MODEL), lambda i: (i, 0))
    c_in, c_out, c_shapes = _cargo_specs(cargo, cargo_layer, steps, lambda i: i)
    out = pl.pallas_call(
        functools.partial(_ffn_kernel, cargo_scales=tuple(sc for _, sc in cargo)),
        grid=(steps,),
        in_specs=[row, _resident_spec(wg), _resident_spec(wu), _resident_spec(wd), _resident_spec(g, layer),
                  _resident_spec(b, layer)] + c_in,
        out_specs=[row] + c_out,
        out_shape=[jax.ShapeDtypeStruct((n, D_MODEL), F32)] + c_shapes,
        scratch_shapes=[pltpu.VMEM((tm // FFN_ROWS, FFN_ROWS, D_FF), BF16)],
        compiler_params=_params("parallel"),
        name="ffn_ln",
    )(x, wg, wu, wd, g, b, *[a for a, _ in cargo])
    return out[0], out[1:]


def _cmul_add(ar, ai, xr, xi, br, bi):
    return ar * xr - ai * xi + br, ar * xi + ai * xr + bi


def _short_conv(bch, cw_ref, cb_ref, zbuf_ref):
    tt = bch.shape[0]
    z = bch[:, D_CONV:2 * D_CONV] * bch[:, 2 * D_CONV:3 * D_CONV]
    zbuf_ref[SUBLANES:SUBLANES + tt, :] = z
    z1 = zbuf_ref[SUBLANES - 1:SUBLANES - 1 + tt, :]
    z2 = zbuf_ref[SUBLANES - 2:SUBLANES - 2 + tt, :]
    y = cw_ref[0:1, :] * z2 + cw_ref[1:2, :] * z1 + cw_ref[2:3, :] * z + cb_ref[...]
    zbuf_ref[0:SUBLANES, :] = zbuf_ref[tt:tt + SUBLANES, :]
    return (bch[:, 0:D_CONV] * y).astype(BF16)


def _s5_scan(u, bbd_ref, astep_ref, aseg_ref, apow_ref, cbd_ref, st_ref, uh_ref, up_ref, xl_ref, xs_ref, side_work):
    for h in range(D_SSM // LANES):
        uh_ref[h] = u[:, h * LANES:(h + 1) * LANES]
        for r in range(SSM_STEPS):
            up_ref[r * SUBLANES:(r + 1) * SUBLANES, h * LANES:(h + 1) * LANES] = (
                uh_ref[h, pl.ds(r, SUBLANES, stride=SSM_STEPS), :])
    upb = up_ref[...].astype(BF16)
    first_row = lax.broadcasted_iota(jnp.int32, (SUBLANES, LANES), 0) == 0
    for c in range(N_CHUNKS):
        cols = slice(c * CHUNK_COLS, (c + 1) * CHUNK_COLS)
        re = slice(c * CHUNK_COLS, c * CHUNK_COLS + LANES)
        im = slice(c * CHUNK_COLS + LANES, (c + 1) * CHUNK_COLS)
        bu = _dot(upb, bbd_ref[:, cols])
        side_work(c)
        ar, ai = astep_ref[:, re], astep_ref[:, im]
        xr, xi = bu[0:SUBLANES, 0:LANES], bu[0:SUBLANES, LANES:CHUNK_COLS]
        xl_ref[0:SUBLANES, re] = xr
        xl_ref[0:SUBLANES, im] = xi
        for r in range(1, SSM_STEPS):
            rows = slice(r * SUBLANES, (r + 1) * SUBLANES)
            xr, xi = _cmul_add(ar, ai, xr, xi, bu[rows, 0:LANES], bu[rows, LANES:CHUNK_COLS])
            xl_ref[rows, re] = xr
            xl_ref[rows, im] = xi
        sr = jnp.where(first_row, st_ref[:, re], pltpu.roll(xr, 1, 0))
        si = jnp.where(first_row, st_ref[:, im], pltpu.roll(xi, 1, 0))
        for s, shift in enumerate((1, 2, 4)):
            rows = slice(s * SUBLANES, (s + 1) * SUBLANES)
            sr, si = _cmul_add(aseg_ref[rows, re], aseg_ref[rows, im],
                               pltpu.roll(sr, shift, 0), pltpu.roll(si, shift, 0), sr, si)
        rows = slice(3 * SUBLANES, 4 * SUBLANES)
        nr, ni = _cmul_add(aseg_ref[rows, re], aseg_ref[rows, im], sr, si, xr, xi)
        st_ref[:, re] = pltpu.roll(nr, 1, 0)
        st_ref[:, im] = pltpu.roll(ni, 1, 0)
        sr2 = jnp.concatenate([sr, sr], axis=0)
        si2 = jnp.concatenate([si, si], axis=0)
        for k in range(SSM_STEPS // 2):
            rows = slice(2 * k * SUBLANES, 2 * (k + 1) * SUBLANES)
            fr, fi = _cmul_add(apow_ref[rows, re], apow_ref[rows, im], sr2, si2, xl_ref[rows, re], xl_ref[rows, im])
            xs_ref[rows, re] = fr.astype(BF16)
            xs_ref[rows, im] = fi.astype(BF16)
    return _dot(xs_ref[...], cbd_ref[...])


def _s5_finish(yp, u, d_ref, wglu_ref, yh_ref):
    halves = D_SSM // LANES
    for h in range(halves):
        for r in range(SSM_STEPS):
            yh_ref[h, pl.ds(r, SUBLANES, stride=SSM_STEPS), :] = (
                yp[r * SUBLANES:(r + 1) * SUBLANES, h * LANES:(h + 1) * LANES])
    y = jnp.concatenate([yh_ref[h] for h in range(halves)], axis=-1) + d_ref[...] * u
    y = jax.nn.gelu(y)
    y = y * jax.nn.sigmoid(_dot(y.astype(BF16), wglu_ref[...]))
    return y.astype(BF16)


def _swa_block(q, k_all, v_all, mask, sink_ref, layer, side_work):
    dn = (((1,), (1,)), ((), ()))
    outs = []
    for h in range(N_KV_HEADS):
        hs = slice(h * HEAD_DIM, (h + 1) * HEAD_DIM)
        heads = range(h * Q_PER_KV, (h + 1) * Q_PER_KV)
        qh = jnp.concatenate([q[:, j * HEAD_DIM:(j + 1) * HEAD_DIM] for j in heads], axis=0)
        s = lax.dot_general(qh, k_all[:, hs], dn, preferred_element_type=F32)
        side_work()
        probs, denoms = [], []
        for g, j in enumerate(heads):
            sink = sink_ref[layer, j]
            sg = jnp.where(mask, s[g * WINDOW:(g + 1) * WINDOW], -jnp.inf)
            m = jnp.maximum(jnp.max(sg, axis=-1, keepdims=True), sink)
            p = jnp.exp(sg - m)
            denoms.append(jnp.sum(p, axis=-1, keepdims=True) + jnp.exp(sink - m))
            probs.append(p.astype(BF16))
        o = _dot(jnp.concatenate(probs, axis=0), v_all[:, hs])
        outs += [o[g * WINDOW:(g + 1) * WINDOW] / denoms[g] for g in range(Q_PER_KV)]
    return jnp.concatenate(outs, axis=-1).astype(BF16)


def _mixer_kernel(sink_ref, x_ref, win_ref, wglu_ref, wc_ref, ws_ref, wa_ref, wo_ref, cw_ref, cb_ref, bbd_ref,
                  astep_ref, aseg_ref, apow_ref, cbd_ref, d_ref, g_ref, b_ref, *rest, layer, cargo_scales):
    n_cargo = len(cargo_scales)
    o_ref = rest[n_cargo]
    zbuf_ref, st_ref, uh_ref, up_ref, xl_ref, xs_ref, yh_ref, kv_ref, gates_ref = rest[2 * n_cargo + 1:]
    first_tile = pl.program_id(1) == 0

    @pl.when(first_tile)
    def _():
        zbuf_ref[0:SUBLANES, :] = jnp.zeros((SUBLANES, D_CONV), F32)
        st_ref[...] = jnp.zeros(st_ref.shape, F32)
        kv_ref[...] = jnp.zeros(kv_ref.shape, BF16)

    row = lax.broadcasted_iota(jnp.int32, (WINDOW, 2 * WINDOW), 0)
    col = lax.broadcasted_iota(jnp.int32, (WINDOW, 2 * WINDOW), 1)
    band = jnp.logical_and(col > row, col - WINDOW <= row)
    first_band = jnp.logical_and(band, jnp.logical_or(col >= WINDOW, jnp.logical_not(first_tile)))

    tt = SSM_TILE
    for sub in range(x_ref.shape[0] // tt):
        tok = slice(sub * tt, (sub + 1) * tt)
        x = x_ref[tok, :]
        xb = x.astype(BF16)

        def proj(lo, hi):
            return _dot(xb, win_ref[:, lo:hi])

        def gate_block(i):
            cols = slice(i * GATE_COLS, (i + 1) * GATE_COLS)
            gates_ref[sub, :, cols] = jax.nn.sigmoid(proj(OFF_G + i * GATE_COLS, OFF_G + (i + 1) * GATE_COLS))

        pending = list(range(N_BRANCH * D_MODEL // GATE_COLS))

        def gates(n):
            for _ in range(min(n, len(pending))):
                gate_block(pending.pop(0))

        u = proj(OFF_U, OFF_Q)
        y_conv = _short_conv(proj(0, OFF_U), cw_ref, cb_ref, zbuf_ref)
        gates(GATES_AHEAD_OF_SCAN)
        yp = _s5_scan(u, bbd_ref, astep_ref, aseg_ref, apow_ref, cbd_ref, st_ref, uh_ref, up_ref, xl_ref, xs_ref,
                      lambda c: gates(GATES_PER_CHUNK))

        q = (proj(OFF_Q, OFF_K) * (HEAD_DIM ** -0.5)).astype(BF16)
        k = jnp.concatenate([kv_ref[0], proj(OFF_K, OFF_V).astype(BF16)], axis=0)
        v = jnp.concatenate([kv_ref[1], proj(OFF_V, OFF_G).astype(BF16)], axis=0)
        kv_ref[0] = k[tt:tt + WINDOW]
        kv_ref[1] = v[tt:tt + WINDOW]
        blocks = []
        for j in range(tt // WINDOW):
            mask = band if (sub or j) else first_band
            blocks.append(_swa_block(q[j * WINDOW:(j + 1) * WINDOW], k[j * WINDOW:(j + 2) * WINDOW],
                                     v[j * WINDOW:(j + 2) * WINDOW], mask, sink_ref, layer, lambda: gates(1)))
        y_attn = jnp.concatenate(blocks, axis=0)
        gates(len(pending))

        y_ssm = _s5_finish(yp, u, d_ref, wglu_ref, yh_ref)

        if sub == 0:
            _narrow_cargo(rest[:n_cargo], rest[n_cargo + 1:2 * n_cargo + 1], cargo_scales)

        t_conv = _dot(y_conv, wc_ref[...])
        t_attn = _dot(y_attn, wa_ref[...])
        t_ssm = _dot(y_ssm, ws_ref[...])
        merged = (gates_ref[sub, :, 0:D_MODEL] * t_conv + gates_ref[sub, :, D_MODEL:2 * D_MODEL] * t_ssm
                  + gates_ref[sub, :, 2 * D_MODEL:3 * D_MODEL] * t_attn)
        mix = _dot(merged.astype(BF16), wo_ref[...])
        o_ref[tok, :] = _layer_norm(ALPHA * x + mix, g_ref[...], b_ref[...])


def _mixer(x, layer, batch, sinks, narrow_w, tables, cargo=()):
    n = x.shape[0]
    tt = SSM_TILE
    tile = MIX_SUBTILES * tt
    per_b = n // batch // tile
    row = pl.BlockSpec((tile, D_MODEL), lambda b, t: (b * per_b + t, 0))
    c_in, c_out, c_shapes = _cargo_specs(cargo, layer, batch * per_b, lambda b, t: b * per_b + t)
    out = pl.pallas_call(
        functools.partial(_mixer_kernel, layer=layer, cargo_scales=tuple(sc for _, sc in cargo)),
        grid=(batch, per_b),
        in_specs=[pl.BlockSpec(memory_space=pltpu.SMEM), row] + [_resident_spec(w) for w in narrow_w]
        + [_resident_spec(w, layer) for w in tables] + c_in,
        out_specs=[row] + c_out,
        out_shape=[jax.ShapeDtypeStruct((n, D_MODEL), F32)] + c_shapes,
        scratch_shapes=[pltpu.VMEM((tt + SUBLANES, D_CONV), F32),
                        pltpu.VMEM((SUBLANES, 2 * N_STATE), F32),
                        pltpu.VMEM((D_SSM // LANES, tt, LANES), F32),
                        pltpu.VMEM((tt, D_SSM), F32),
                        pltpu.VMEM((tt, 2 * N_STATE), F32),
                        pltpu.VMEM((tt, 2 * N_STATE), BF16),
                        pltpu.VMEM((D_SSM // LANES, tt, LANES), F32),
                        pltpu.VMEM((2, WINDOW, D_KV), BF16),
                        pltpu.VMEM((MIX_SUBTILES, tt, N_BRANCH * D_MODEL), F32)],
        compiler_params=_params("arbitrary", "arbitrary"),
        name="mixer",
    )(sinks, x, *narrow_w, *tables, *[a for a, _ in cargo])
    return out[0], out[1:]


def _chunked(re, im, axis):
    def split(a):
        return a.reshape(a.shape[:axis] + (N_CHUNKS, 1, LANES) + a.shape[axis + 1:])

    both = jnp.concatenate([split(re), split(im)], axis=axis + 1)
    return both.reshape(re.shape[:axis] + (2 * N_STATE,) + re.shape[axis + 1:])


def _ssm_tables(a_re, a_im, log_dt, b_re, b_im, c_re, c_im):
    dt = jnp.exp(log_dt)[..., None]
    mag = jnp.exp(a_re * dt)
    ang = a_im * dt
    abar_re = mag * jnp.cos(ang)
    abar_im = mag * jnp.sin(ang)
    nr = abar_re - 1.0
    ni = abar_im
    den = a_re * a_re + a_im * a_im
    coef_re = (nr * a_re + ni * a_im) / den
    coef_im = (ni * a_re - nr * a_im) / den
    bbar_re = coef_re[..., None] * b_re - coef_im[..., None] * b_im
    bbar_im = coef_re[..., None] * b_im + coef_im[..., None] * b_re

    eye = jnp.eye(N_SSM_GROUPS, dtype=F32)

    def in_blockdiag(w):
        return jnp.einsum('lgpc,gh->lgchp', w, eye).reshape(DEPTH, D_SSM, N_STATE)

    def out_blockdiag(w):
        return jnp.einsum('lgcp,gh->lgphc', w, eye).reshape(DEPTH, N_STATE, D_SSM)

    bbd = _chunked(in_blockdiag(bbar_re), in_blockdiag(bbar_im), axis=2).astype(BF16)
    cbd = _chunked(out_blockdiag(c_re), -out_blockdiag(c_im), axis=1).astype(BF16)

    pr, pi = abar_re.reshape(DEPTH, 1, N_STATE), abar_im.reshape(DEPTH, 1, N_STATE)
    while pr.shape[1] < SSM_STEPS:
        tr, ti = pr[:, -1:], pi[:, -1:]
        pr, pi = (jnp.concatenate([pr, pr * tr - pi * ti], axis=1),
                  jnp.concatenate([pi, pr * ti + pi * tr], axis=1))

    def bcast(a, rows):
        return jnp.broadcast_to(a, (DEPTH, rows, N_STATE))

    astep = _chunked(bcast(pr[:, 0:1], SUBLANES), bcast(pi[:, 0:1], SUBLANES), axis=2)
    apow = _chunked(jnp.repeat(pr, SUBLANES, axis=1), jnp.repeat(pi, SUBLANES, axis=1), axis=2)

    sublane = jnp.arange(SUBLANES)[None, :, None]
    qr, qi = pr[:, -1:], pi[:, -1:]
    seg_re, seg_im = [], []
    for shift in (1, 2, 4):
        seg_re.append(jnp.where(sublane >= shift, bcast(qr, SUBLANES), 0.0))
        seg_im.append(jnp.where(sublane >= shift, bcast(qi, SUBLANES), 0.0))
        qr, qi = qr * qr - qi * qi, 2.0 * qr * qi
    seg_re.append(bcast(pr[:, -1:], SUBLANES))
    seg_im.append(bcast(pi[:, -1:], SUBLANES))
    aseg = _chunked(jnp.concatenate(seg_re, axis=1), jnp.concatenate(seg_im, axis=1), axis=2)
    return bbd, astep, aseg, apow, cbd


def _vec(p):
    return p.reshape(DEPTH, 1, -1)


MIXER_NARROW = ('w_in', 'ssm_w_glu', 'w_br_conv', 'w_br_ssm', 'w_br_attn', 'w_out')


def _mixer_tables(p):
    bbd, astep, aseg, apow, cbd = _ssm_tables(p['ssm_a_re'], p['ssm_a_im'], p['ssm_log_dt'], p['ssm_b_re'],
                                              p['ssm_b_im'], p['ssm_c_re'], p['ssm_c_im'])
    return (p['conv_w'], _vec(p['conv_b']), bbd, astep, aseg, apow, cbd, _vec(p['ssm_d']),
            _vec(p['ln2_g']), _vec(p['ln2_b']))


def kernel(x, ffn1_w_gate, ffn1_w_up, ffn1_w_down, ln1_g, ln1_b, w_in, conv_w, conv_b, ssm_a_re, ssm_a_im, ssm_log_dt, ssm_b_re, ssm_b_im, ssm_c_re, ssm_c_im, ssm_d, ssm_w_glu, attn_sinks, w_br_conv, w_br_ssm, w_br_attn, w_out, ln2_g, ln2_b, ffn2_w_gate, ffn2_w_up, ffn2_w_down, ln3_g, ln3_b):
    batch, seq, _ = x.shape
    h = x.reshape(batch * seq, D_MODEL)
    p = dict(w_in=w_in, conv_w=conv_w, conv_b=conv_b, ssm_a_re=ssm_a_re, ssm_a_im=ssm_a_im, ssm_log_dt=ssm_log_dt,
             ssm_b_re=ssm_b_re, ssm_b_im=ssm_b_im, ssm_c_re=ssm_c_re, ssm_c_im=ssm_c_im, ssm_d=ssm_d,
             ssm_w_glu=ssm_w_glu, w_br_conv=w_br_conv, w_br_ssm=w_br_ssm, w_br_attn=w_br_attn, w_out=w_out,
             ln2_g=ln2_g, ln2_b=ln2_b)
    tables = _mixer_tables(p)
    mixer_cargo = [(p[name], 1.0) for name in MIXER_NARROW]
    ffn1_cargo = [(ffn1_w_gate, 1.0), (ffn1_w_up, 1.0), (ffn1_w_down, 0.5)]
    ffn2_cargo = [(ffn2_w_gate, 1.0), (ffn2_w_up, 1.0), (ffn2_w_down, 0.5)]
    ffn_w = (ffn1_w_gate[0].astype(BF16), ffn1_w_up[0].astype(BF16), (0.5 * ffn1_w_down[0]).astype(BF16))
    ln1, ln3 = (_vec(ln1_g), _vec(ln1_b)), (_vec(ln3_g), _vec(ln3_b))
    for l in range(DEPTH):
        h, mix_w = _ffn_ln(h, l, *ffn_w, *ln1, cargo=mixer_cargo, cargo_layer=l)
        h, ffn_w = _mixer(h, l, batch, attn_sinks, mix_w, tables, cargo=ffn2_cargo)
        last = l == DEPTH - 1
        h, ffn_w = _ffn_ln(h, l, *ffn_w, *ln3, cargo=() if last else ffn1_cargo, cargo_layer=l + 1)
    return h.reshape(batch, seq, D_MODEL)
```

```python
import functools

import jax
import jax.numpy as jnp
from jax import lax
from jax.experimental import pallas as pl
from jax.experimental.pallas import tpu as pltpu

D_MODEL = 1024
DEPTH = 4
D_CONV = 256
CONV_WIDTH = 3
D_SSM = 256
SSM_GROUP = 16
N_SSM_GROUPS = D_SSM // SSM_GROUP
SSM_STATE = 64
N_STATE = N_SSM_GROUPS * SSM_STATE
N_Q_HEADS = 8
N_KV_HEADS = 2
Q_PER_KV = N_Q_HEADS // N_KV_HEADS
HEAD_DIM = 64
D_ATTN = N_Q_HEADS * HEAD_DIM
D_KV = N_KV_HEADS * HEAD_DIM
WINDOW = 128
N_BRANCH = 3
D_FF = 2816
ALPHA = (2 * DEPTH) ** 0.25
LN_EPS = 1e-5
D_IN = 3 * D_CONV + D_SSM + D_ATTN + 2 * D_KV + N_BRANCH * D_MODEL

OFF_U = 3 * D_CONV
OFF_Q = OFF_U + D_SSM
OFF_K = OFF_Q + D_ATTN
OFF_V = OFF_K + D_KV
OFF_G = OFF_V + D_KV

SUBLANES = 8
LANES = 128
BF16_ROWS = 16
FF_CHUNK = 256
FFN_ROWS = 512
DOWN_ROWS = 256
VMEM_LIMIT = 56 * 1024 * 1024

SSM_TILE = 256
SSM_STEPS = SSM_TILE // SUBLANES
N_CHUNKS = N_STATE // LANES
CHUNK_COLS = 2 * LANES
GATE_COLS = 256
MIX_SUBTILES = 2
GATES_AHEAD_OF_SCAN = 2
GATES_PER_CHUNK = 1

F32 = jnp.float32
BF16 = jnp.bfloat16


def _dot(a, b):
    return jnp.dot(a, b, preferred_element_type=F32)


def _layer_norm(r, g, b):
    mu = jnp.mean(r, axis=-1, keepdims=True)
    d = r - mu
    var = jnp.mean(d * d, axis=-1, keepdims=True)
    return d * lax.rsqrt(var + LN_EPS) * g + b


def _resident_spec(arr, layer=None):
    if layer is None:
        return pl.BlockSpec(arr.shape, lambda *_: (0,) * arr.ndim, pipeline_mode=pl.Buffered(1))
    zeros = (0,) * (arr.ndim - 1)
    return pl.BlockSpec((None,) + arr.shape[1:], lambda *_: (layer,) + zeros, pipeline_mode=pl.Buffered(1))


def _cargo_specs(cargo, layer, steps, step_of):
    in_specs, out_specs, out_shapes = [], [], []
    for arr, _ in cargo:
        _, rows, cols = arr.shape
        blocks = steps if rows % (steps * BF16_ROWS) == 0 else steps // 2
        assert rows % (blocks * BF16_ROWS) == 0 and steps % blocks == 0, (arr.shape, steps)
        per = steps // blocks
        in_specs.append(pl.BlockSpec((None, rows // blocks, cols),
                                     lambda *ids, per=per: (layer, step_of(*ids) // per, 0)))
        out_specs.append(pl.BlockSpec((rows // blocks, cols), lambda *ids, per=per: (step_of(*ids) // per, 0)))
        out_shapes.append(jax.ShapeDtypeStruct((rows, cols), BF16))
    return in_specs, out_specs, out_shapes


def _narrow_cargo(in_refs, out_refs, scales):
    for src, dst, scale in zip(in_refs, out_refs, scales):
        w = src[...]
        dst[...] = (w if scale == 1.0 else w * scale).astype(BF16)


def _params(*sem):
    return pltpu.CompilerParams(dimension_semantics=sem, vmem_limit_bytes=VMEM_LIMIT)


def _ffn_kernel(x_ref, wg_ref, wu_ref, wd_ref, g_ref, b_ref, *rest, cargo_scales):
    n_cargo = len(cargo_scales)
    o_ref = rest[n_cargo]
    h_ref = rest[-1]
    n_chunks = D_FF // FF_CHUNK
    xb_all = x_ref[...].astype(BF16)
    for r in range(x_ref.shape[0] // FFN_ROWS):
        rows = slice(r * FFN_ROWS, (r + 1) * FFN_ROWS)
        xb = xb_all[rows, :]
        for c in range(n_chunks):
            cols = slice(c * FF_CHUNK, (c + 1) * FF_CHUNK)
            gate = _dot(xb, wg_ref[:, cols])
            up = _dot(xb, wu_ref[:, cols])
            h_ref[r, :, cols] = (jax.nn.silu(gate) * up).astype(BF16)
            if r == 0 and c == n_chunks // 2:
                _narrow_cargo(rest[:n_cargo], rest[n_cargo + 1:2 * n_cargo + 1], cargo_scales)
        for rb in range(FFN_ROWS // DOWN_ROWS):
            blk = slice(r * FFN_ROWS + rb * DOWN_ROWS, r * FFN_ROWS + (rb + 1) * DOWN_ROWS)
            acc = _dot(h_ref[r, rb * DOWN_ROWS:(rb + 1) * DOWN_ROWS, :], wd_ref[...])
            o_ref[blk, :] = _layer_norm(ALPHA * x_ref[blk, :] + acc, g_ref[...], b_ref[...])


def _ffn_ln(x, layer, wg, wu, wd, g, b, cargo=(), cargo_layer=0, tm=2 * FFN_ROWS):
    n = x.shape[0]
    steps = n // tm
    row = pl.BlockSpec((tm, D_MODEL), lambda i: (i, 0))
    c_in, c_out, c_shapes = _cargo_specs(cargo, cargo_layer, steps, lambda i: i)
    out = pl.pallas_call(
        functools.partial(_ffn_kernel, cargo_scales=tuple(sc for _, sc in cargo)),
        grid=(steps,),
        in_specs=[row, _resident_spec(wg), _resident_spec(wu), _resident_spec(wd), _resident_spec(g, layer),
                  _resident_spec(b, layer)] + c_in,
        out_specs=[row] + c_out,
        out_shape=[jax.ShapeDtypeStruct((n, D_MODEL), F32)] + c_shapes,
        scratch_shapes=[pltpu.VMEM((tm // FFN_ROWS, FFN_ROWS, D_FF), BF16)],
        compiler_params=_params("parallel"),
        name="ffn_ln",
    )(x, wg, wu, wd, g, b, *[a for a, _ in cargo])
    return out[0], out[1:]


def _cmul_add(ar, ai, xr, xi, br, bi):
    return ar * xr - ai * xi + br, ar * xi + ai * xr + bi


def _short_conv(bch, cw_ref, cb_ref, zbuf_ref):
    tt = bch.shape[0]
    z = bch[:, D_CONV:2 * D_CONV] * bch[:, 2 * D_CONV:3 * D_CONV]
    zbuf_ref[SUBLANES:SUBLANES + tt, :] = z
    z1 = zbuf_ref[SUBLANES - 1:SUBLANES - 1 + tt, :]
    z2 = zbuf_ref[SUBLANES - 2:SUBLANES - 2 + tt, :]
    y = cw_ref[0:1, :] * z2 + cw_ref[1:2, :] * z1 + cw_ref[2:3, :] * z + cb_ref[...]
    zbuf_ref[0:SUBLANES, :] = zbuf_ref[tt:tt + SUBLANES, :]
    return (bch[:, 0:D_CONV] * y).astype(BF16)


def _s5_scan(u, bbd_ref, astep_ref, aseg_ref, apow_ref, cbd_ref, st_ref, uh_ref, up_ref, xl_ref, xs_ref, side_work):
    for h in range(D_SSM // LANES):
        uh_ref[h] = u[:, h * LANES:(h + 1) * LANES]
        for r in range(SSM_STEPS):
            up_ref[r * SUBLANES:(r + 1) * SUBLANES, h * LANES:(h + 1) * LANES] = (
                uh_ref[h, pl.ds(r, SUBLANES, stride=SSM_STEPS), :])
    upb = up_ref[...].astype(BF16)
    first_row = lax.broadcasted_iota(jnp.int32, (SUBLANES, LANES), 0) == 0
    for c in range(N_CHUNKS):
        cols = slice(c * CHUNK_COLS, (c + 1) * CHUNK_COLS)
        re = slice(c * CHUNK_COLS, c * CHUNK_COLS + LANES)
        im = slice(c * CHUNK_COLS + LANES, (c + 1) * CHUNK_COLS)
        bu = _dot(upb, bbd_ref[:, cols])
        side_work(c)
        ar, ai = astep_ref[:, re], astep_ref[:, im]
        xr, xi = bu[0:SUBLANES, 0:LANES], bu[0:SUBLANES, LANES:CHUNK_COLS]
        xl_ref[0:SUBLANES, re] = xr
        xl_ref[0:SUBLANES, im] = xi
        for r in range(1, SSM_STEPS):
            rows = slice(r * SUBLANES, (r + 1) * SUBLANES)
            xr, xi = _cmul_add(ar, ai, xr, xi, bu[rows, 0:LANES], bu[rows, LANES:CHUNK_COLS])
            xl_ref[rows, re] = xr
            xl_ref[rows, im] = xi
        sr = jnp.where(first_row, st_ref[:, re], pltpu.roll(xr, 1, 0))
        si = jnp.where(first_row, st_ref[:, im], pltpu.roll(xi, 1, 0))
        for s, shift in enumerate((1, 2, 4)):
            rows = slice(s * SUBLANES, (s + 1) * SUBLANES)
            sr, si = _cmul_add(aseg_ref[rows, re], aseg_ref[rows, im],
                               pltpu.roll(sr, shift, 0), pltpu.roll(si, shift, 0), sr, si)
        rows = slice(3 * SUBLANES, 4 * SUBLANES)
        nr, ni = _cmul_add(aseg_ref[rows, re], aseg_ref[rows, im], sr, si, xr, xi)
        st_ref[:, re] = pltpu.roll(nr, 1, 0)
        st_ref[:, im] = pltpu.roll(ni, 1, 0)
        sr2 = jnp.concatenate([sr, sr], axis=0)
        si2 = jnp.concatenate([si, si], axis=0)
        for k in range(SSM_STEPS // 2):
            rows = slice(2 * k * SUBLANES, 2 * (k + 1) * SUBLANES)
            fr, fi = _cmul_add(apow_ref[rows, re], apow_ref[rows, im], sr2, si2, xl_ref[rows, re], xl_ref[rows, im])
            xs_ref[rows, re] = fr.astype(BF16)
            xs_ref[rows, im] = fi.astype(BF16)
    return _dot(xs_ref[...], cbd_ref[...])


def _s5_finish(yp, u, d_ref, wglu_ref, yh_ref):
    halves = D_SSM // LANES
    for h in range(halves):
        for r in range(SSM_STEPS):
            yh_ref[h, pl.ds(r, SUBLANES, stride=SSM_STEPS), :] = (
                yp[r * SUBLANES:(r + 1) * SUBLANES, h * LANES:(h + 1) * LANES])
    y = jnp.concatenate([yh_ref[h] for h in range(halves)], axis=-1) + d_ref[...] * u
    y = jax.nn.gelu(y)
    y = y * jax.nn.sigmoid(_dot(y.astype(BF16), wglu_ref[...]))
    return y.astype(BF16)


def _swa_heads(q, k_all, v_all, mask, h, sink_ref, layer):
    dn = (((1,), (1,)), ((), ()))
    hs = slice(h * HEAD_DIM, (h + 1) * HEAD_DIM)
    heads = range(h * Q_PER_KV, (h + 1) * Q_PER_KV)
    qh = jnp.concatenate([q[:, j * HEAD_DIM:(j + 1) * HEAD_DIM] for j in heads], axis=0)
    s = lax.dot_general(qh, k_all[:, hs], dn, preferred_element_type=F32)
    probs, denoms = [], []
    for g, j in enumerate(heads):
        sink = sink_ref[layer, j]
        sg = jnp.where(mask, s[g * WINDOW:(g + 1) * WINDOW], -jnp.inf)
        m = jnp.maximum(jnp.max(sg, axis=-1, keepdims=True), sink)
        p = jnp.exp(sg - m)
        denoms.append(jnp.sum(p, axis=-1, keepdims=True) + jnp.exp(sink - m))
        probs.append(p.astype(BF16))
    o = _dot(jnp.concatenate(probs, axis=0), v_all[:, hs])
    return jnp.concatenate([o[g * WINDOW:(g + 1) * WINDOW] / denoms[g] for g in range(Q_PER_KV)],
                           axis=-1).astype(BF16)


def _mixer_kernel(sink_ref, x_ref, win_ref, wglu_ref, wc_ref, ws_ref, wa_ref, wo_ref, cw_ref, cb_ref, bbd_ref,
                  astep_ref, aseg_ref, apow_ref, cbd_ref, d_ref, g_ref, b_ref, *rest, layer, cargo_scales):
    n_cargo = len(cargo_scales)
    o_ref = rest[n_cargo]
    zbuf_ref, st_ref, uh_ref, up_ref, xl_ref, xs_ref, yh_ref, kv_ref, gates_ref = rest[2 * n_cargo + 1:]
    first_tile = pl.program_id(1) == 0

    @pl.when(first_tile)
    def _():
        zbuf_ref[0:SUBLANES, :] = jnp.zeros((SUBLANES, D_CONV), F32)
        st_ref[...] = jnp.zeros(st_ref.shape, F32)
        kv_ref[...] = jnp.zeros(kv_ref.shape, BF16)

    row = lax.broadcasted_iota(jnp.int32, (WINDOW, 2 * WINDOW), 0)
    col = lax.broadcasted_iota(jnp.int32, (WINDOW, 2 * WINDOW), 1)
    band = jnp.logical_and(col > row, col - WINDOW <= row)
    first_band = jnp.logical_and(band, jnp.logical_or(col >= WINDOW, jnp.logical_not(first_tile)))

    tt = SSM_TILE
    for sub in range(x_ref.shape[0] // tt):
        tok = slice(sub * tt, (sub + 1) * tt)
        x = x_ref[tok, :]
        xb = x.astype(BF16)

        def proj(lo, hi):
            return _dot(xb, win_ref[:, lo:hi])

        def gate_block(i):
            cols = slice(i * GATE_COLS, (i + 1) * GATE_COLS)
            gates_ref[sub, :, cols] = jax.nn.sigmoid(proj(OFF_G + i * GATE_COLS, OFF_G + (i + 1) * GATE_COLS))

        pending = list(range(N_BRANCH * D_MODEL // GATE_COLS))

        def gates(n):
            for _ in range(min(n, len(pending))):
                gate_block(pending.pop(0))

        u = proj(OFF_U, OFF_Q)
        y_conv = _short_conv(proj(0, OFF_U), cw_ref, cb_ref, zbuf_ref)
        q = (proj(OFF_Q, OFF_K) * (HEAD_DIM ** -0.5)).astype(BF16)
        k = jnp.concatenate([kv_ref[0], proj(OFF_K, OFF_V).astype(BF16)], axis=0)
        v = jnp.concatenate([kv_ref[1], proj(OFF_V, OFF_G).astype(BF16)], axis=0)
        kv_ref[0] = k[tt:tt + WINDOW]
        kv_ref[1] = v[tt:tt + WINDOW]
        attn = {}

        def attention_unit(i):
            j, h = divmod(i, N_KV_HEADS)
            attn[i] = _swa_heads(q[j * WINDOW:(j + 1) * WINDOW], k[j * WINDOW:(j + 2) * WINDOW],
                                 v[j * WINDOW:(j + 2) * WINDOW], band if (sub or j) else first_band, h,
                                 sink_ref, layer)

        def side_work(c):
            if c % 2 == 0:
                attention_unit(c // 2)
            else:
                gates(GATES_PER_CHUNK)

        gates(GATES_AHEAD_OF_SCAN)
        yp = _s5_scan(u, bbd_ref, astep_ref, aseg_ref, apow_ref, cbd_ref, st_ref, uh_ref, up_ref, xl_ref, xs_ref,
                      side_work)
        y_attn = jnp.concatenate(
            [jnp.concatenate([attn[j * N_KV_HEADS + h] for h in range(N_KV_HEADS)], axis=-1)
             for j in range(tt // WINDOW)], axis=0)
        gates(len(pending))

        y_ssm = _s5_finish(yp, u, d_ref, wglu_ref, yh_ref)

        if sub == 0:
            _narrow_cargo(rest[:n_cargo], rest[n_cargo + 1:2 * n_cargo + 1], cargo_scales)

        t_conv = _dot(y_conv, wc_ref[...])
        t_attn = _dot(y_attn, wa_ref[...])
        t_ssm = _dot(y_ssm, ws_ref[...])
        merged = (gates_ref[sub, :, 0:D_MODEL] * t_conv + gates_ref[sub, :, D_MODEL:2 * D_MODEL] * t_ssm
                  + gates_ref[sub, :, 2 * D_MODEL:3 * D_MODEL] * t_attn)
        mix = _dot(merged.astype(BF16), wo_ref[...])
        o_ref[tok, :] = _layer_norm(ALPHA * x + mix, g_ref[...], b_ref[...])


def _mixer(x, layer, batch, sinks, narrow_w, tables, cargo=()):
    n = x.shape[0]
    tt = SSM_TILE
    tile = MIX_SUBTILES * tt
    per_b = n // batch // tile
    row = pl.BlockSpec((tile, D_MODEL), lambda b, t: (b * per_b + t, 0))
    c_in, c_out, c_shapes = _cargo_specs(cargo, layer, batch * per_b, lambda b, t: b * per_b + t)
    out = pl.pallas_call(
        functools.partial(_mixer_kernel, layer=layer, cargo_scales=tuple(sc for _, sc in cargo)),
        grid=(batch, per_b),
        in_specs=[pl.BlockSpec(memory_space=pltpu.SMEM), row] + [_resident_spec(w) for w in narrow_w]
        + [_resident_spec(w, layer) for w in tables] + c_in,
        out_specs=[row] + c_out,
        out_shape=[jax.ShapeDtypeStruct((n, D_MODEL), F32)] + c_shapes,
        scratch_shapes=[pltpu.VMEM((tt + SUBLANES, D_CONV), F32),
                        pltpu.VMEM((SUBLANES, 2 * N_STATE), F32),
                        pltpu.VMEM((D_SSM // LANES, tt, LANES), F32),
                        pltpu.VMEM((tt, D_SSM), F32),
                        pltpu.VMEM((tt, 2 * N_STATE), F32),
                        pltpu.VMEM((tt, 2 * N_STATE), BF16),
                        pltpu.VMEM((D_SSM // LANES, tt, LANES), F32),
                        pltpu.VMEM((2, WINDOW, D_KV), BF16),
                        pltpu.VMEM((MIX_SUBTILES, tt, N_BRANCH * D_MODEL), F32)],
        compiler_params=_params("arbitrary", "arbitrary"),
        name="mixer",
    )(sinks, x, *narrow_w, *tables, *[a for a, _ in cargo])
    return out[0], out[1:]


def _chunked(re, im, axis):
    def split(a):
        return a.reshape(a.shape[:axis] + (N_CHUNKS, 1, LANES) + a.shape[axis + 1:])

    both = jnp.concatenate([split(re), split(im)], axis=axis + 1)
    return both.reshape(re.shape[:axis] + (2 * N_STATE,) + re.shape[axis + 1:])


def _ssm_tables(a_re, a_im, log_dt, b_re, b_im, c_re, c_im):
    dt = jnp.exp(log_dt)[..., None]
    mag = jnp.exp(a_re * dt)
    ang = a_im * dt
    abar_re = mag * jnp.cos(ang)
    abar_im = mag * jnp.sin(ang)
    nr = abar_re - 1.0
    ni = abar_im
    den = a_re * a_re + a_im * a_im
    coef_re = (nr * a_re + ni * a_im) / den
    coef_im = (ni * a_re - nr * a_im) / den
    bbar_re = coef_re[..., None] * b_re - coef_im[..., None] * b_im
    bbar_im = coef_re[..., None] * b_im + coef_im[..., None] * b_re

    eye = jnp.eye(N_SSM_GROUPS, dtype=F32)

    def in_blockdiag(w):
        return jnp.einsum('lgpc,gh->lgchp', w, eye).reshape(DEPTH, D_SSM, N_STATE)

    def out_blockdiag(w):
        return jnp.einsum('lgcp,gh->lgphc', w, eye).reshape(DEPTH, N_STATE, D_SSM)

    bbd = _chunked(in_blockdiag(bbar_re), in_blockdiag(bbar_im), axis=2).astype(BF16)
    cbd = _chunked(out_blockdiag(c_re), -out_blockdiag(c_im), axis=1).astype(BF16)

    pr, pi = abar_re.reshape(DEPTH, 1, N_STATE), abar_im.reshape(DEPTH, 1, N_STATE)
    while pr.shape[1] < SSM_STEPS:
        tr, ti = pr[:, -1:], pi[:, -1:]
        pr, pi = (jnp.concatenate([pr, pr * tr - pi * ti], axis=1),
                  jnp.concatenate([pi, pr * ti + pi * tr], axis=1))

    def bcast(a, rows):
        return jnp.broadcast_to(a, (DEPTH, rows, N_STATE))

    astep = _chunked(bcast(pr[:, 0:1], SUBLANES), bcast(pi[:, 0:1], SUBLANES), axis=2)
    apow = _chunked(jnp.repeat(pr, SUBLANES, axis=1), jnp.repeat(pi, SUBLANES, axis=1), axis=2)

    sublane = jnp.arange(SUBLANES)[None, :, None]
    qr, qi = pr[:, -1:], pi[:, -1:]
    seg_re, seg_im = [], []
    for shift in (1, 2, 4):
        seg_re.append(jnp.where(sublane >= shift, bcast(qr, SUBLANES), 0.0))
        seg_im.append(jnp.where(sublane >= shift, bcast(qi, SUBLANES), 0.0))
        qr, qi = qr * qr - qi * qi, 2.0 * qr * qi
    seg_re.append(bcast(pr[:, -1:], SUBLANES))
    seg_im.append(bcast(pi[:, -1:], SUBLANES))
    aseg = _chunked(jnp.concatenate(seg_re, axis=1), jnp.concatenate(seg_im, axis=1), axis=2)
    return bbd, astep, aseg, apow, cbd


def _vec(p):
    return p.reshape(DEPTH, 1, -1)


MIXER_NARROW = ('w_in', 'ssm_w_glu', 'w_br_conv', 'w_br_ssm', 'w_br_attn', 'w_out')


def _mixer_tables(p):
    bbd, astep, aseg, apow, cbd = _ssm_tables(p['ssm_a_re'], p['ssm_a_im'], p['ssm_log_dt'], p['ssm_b_re'],
                                              p['ssm_b_im'], p['ssm_c_re'], p['ssm_c_im'])
    return (p['conv_w'], _vec(p['conv_b']), bbd, astep, aseg, apow, cbd, _vec(p['ssm_d']),
            _vec(p['ln2_g']), _vec(p['ln2_b']))


def kernel(x, ffn1_w_gate, ffn1_w_up, ffn1_w_down, ln1_g, ln1_b, w_in, conv_w, conv_b, ssm_a_re, ssm_a_im, ssm_log_dt, ssm_b_re, ssm_b_im, ssm_c_re, ssm_c_im, ssm_d, ssm_w_glu, attn_sinks, w_br_conv, w_br_ssm, w_br_attn, w_out, ln2_g, ln2_b, ffn2_w_gate, ffn2_w_up, ffn2_w_down, ln3_g, ln3_b):
    batch, seq, _ = x.shape
    h = x.reshape(batch * seq, D_MODEL)
    p = dict(w_in=w_in, conv_w=conv_w, conv_b=conv_b, ssm_a_re=ssm_a_re, ssm_a_im=ssm_a_im, ssm_log_dt=ssm_log_dt,
             ssm_b_re=ssm_b_re, ssm_b_im=ssm_b_im, ssm_c_re=ssm_c_re, ssm_c_im=ssm_c_im, ssm_d=ssm_d,
             ssm_w_glu=ssm_w_glu, w_br_conv=w_br_conv, w_br_ssm=w_br_ssm, w_br_attn=w_br_attn, w_out=w_out,
             ln2_g=ln2_g, ln2_b=ln2_b)
    tables = _mixer_tables(p)
    mixer_cargo = [(p[name], 1.0) for name in MIXER_NARROW]
    ffn1_cargo = [(ffn1_w_gate, 1.0), (ffn1_w_up, 1.0), (ffn1_w_down, 0.5)]
    ffn2_cargo = [(ffn2_w_gate, 1.0), (ffn2_w_up, 1.0), (ffn2_w_down, 0.5)]
    ffn_w = (ffn1_w_gate[0].astype(BF16), ffn1_w_up[0].astype(BF16), (0.5 * ffn1_w_down[0]).astype(BF16))
    ln1, ln3 = (_vec(ln1_g), _vec(ln1_b)), (_vec(ln3_g), _vec(ln3_b))
    for l in range(DEPTH):
        h, mix_w = _ffn_ln(h, l, *ffn_w, *ln1, cargo=mixer_cargo, cargo_layer=l)
        h, ffn_w = _mixer(h, l, batch, attn_sinks, mix_w, tables, cargo=ffn2_cargo)
        last = l == DEPTH - 1
        h, ffn_w = _ffn_ln(h, l, *ffn_w, *ln3, cargo=() if last else ffn1_cargo, cargo_layer=l + 1)
    return h.reshape(batch, seq, D_MODEL)
```

```python
import functools

import jax
import jax.numpy as jnp
from jax import lax
from jax.experimental import pallas as pl
from jax.experimental.pallas import tpu as pltpu

D_MODEL = 1024
DEPTH = 4
D_CONV = 256
CONV_WIDTH = 3
D_SSM = 256
SSM_GROUP = 16
N_SSM_GROUPS = D_SSM // SSM_GROUP
SSM_STATE = 64
N_STATE = N_SSM_GROUPS * SSM_STATE
N_Q_HEADS = 8
N_KV_HEADS = 2
Q_PER_KV = N_Q_HEADS // N_KV_HEADS
HEAD_DIM = 64
D_ATTN = N_Q_HEADS * HEAD_DIM
D_KV = N_KV_HEADS * HEAD_DIM
WINDOW = 128
N_BRANCH = 3
D_FF = 2816
ALPHA = (2 * DEPTH) ** 0.25
LN_EPS = 1e-5
D_IN = 3 * D_CONV + D_SSM + D_ATTN + 2 * D_KV + N_BRANCH * D_MODEL

OFF_U = 3 * D_CONV
OFF_Q = OFF_U + D_SSM
OFF_K = OFF_Q + D_ATTN
OFF_V = OFF_K + D_KV
OFF_G = OFF_V + D_KV

SUBLANES = 8
LANES = 128
BF16_ROWS = 16
FF_CHUNK = 256
FFN_ROWS = 512
DOWN_ROWS = 256
VMEM_LIMIT = 56 * 1024 * 1024

SSM_TILE = 256
SSM_STEPS = SSM_TILE // SUBLANES
N_CHUNKS = N_STATE // LANES
CHUNK_COLS = 2 * LANES
GROUPS_PER_CHUNK = LANES // SSM_STATE
GATE_COLS = 256
MIX_SUBTILES = 2
GATES_AHEAD_OF_SCAN = 2
GATES_PER_CHUNK = 1

F32 = jnp.float32
BF16 = jnp.bfloat16


def _dot(a, b):
    return jnp.dot(a, b, preferred_element_type=F32)


def _layer_norm(r, g, b):
    mu = jnp.mean(r, axis=-1, keepdims=True)
    d = r - mu
    var = jnp.mean(d * d, axis=-1, keepdims=True)
    return d * lax.rsqrt(var + LN_EPS) * g + b


def _resident_spec(arr, layer=None):
    if layer is None:
        return pl.BlockSpec(arr.shape, lambda *_: (0,) * arr.ndim, pipeline_mode=pl.Buffered(1))
    zeros = (0,) * (arr.ndim - 1)
    return pl.BlockSpec((None,) + arr.shape[1:], lambda *_: (layer,) + zeros, pipeline_mode=pl.Buffered(1))


def _cargo_specs(cargo, layer, steps, step_of):
    in_specs, out_specs, out_shapes = [], [], []
    for arr, _ in cargo:
        _, rows, cols = arr.shape
        blocks = steps if rows % (steps * BF16_ROWS) == 0 else steps // 2
        assert rows % (blocks * BF16_ROWS) == 0 and steps % blocks == 0, (arr.shape, steps)
        per = steps // blocks
        in_specs.append(pl.BlockSpec((None, rows // blocks, cols),
                                     lambda *ids, per=per: (layer, step_of(*ids) // per, 0)))
        out_specs.append(pl.BlockSpec((rows // blocks, cols), lambda *ids, per=per: (step_of(*ids) // per, 0)))
        out_shapes.append(jax.ShapeDtypeStruct((rows, cols), BF16))
    return in_specs, out_specs, out_shapes


def _narrow_cargo(in_refs, out_refs, scales):
    for src, dst, scale in zip(in_refs, out_refs, scales):
        w = src[...]
        dst[...] = (w if scale == 1.0 else w * scale).astype(BF16)


def _params(*sem):
    return pltpu.CompilerParams(dimension_semantics=sem, vmem_limit_bytes=VMEM_LIMIT)


def _ffn_kernel(x_ref, wg_ref, wu_ref, wd_ref, g_ref, b_ref, *rest, cargo_scales):
    n_cargo = len(cargo_scales)
    o_ref = rest[n_cargo]
    h_ref = rest[-1]
    n_chunks = D_FF // FF_CHUNK
    xb_all = x_ref[...].astype(BF16)
    for r in range(x_ref.shape[0] // FFN_ROWS):
        rows = slice(r * FFN_ROWS, (r + 1) * FFN_ROWS)
        xb = xb_all[rows, :]
        for c in range(n_chunks):
            cols = slice(c * FF_CHUNK, (c + 1) * FF_CHUNK)
            gate = _dot(xb, wg_ref[:, cols])
            up = _dot(xb, wu_ref[:, cols])
            h_ref[r, :, cols] = (jax.nn.silu(gate) * up).astype(BF16)
            if r == 0 and c == n_chunks // 2:
                _narrow_cargo(rest[:n_cargo], rest[n_cargo + 1:2 * n_cargo + 1], cargo_scales)
        for rb in range(FFN_ROWS // DOWN_ROWS):
            blk = slice(r * FFN_ROWS + rb * DOWN_ROWS, r * FFN_ROWS + (rb + 1) * DOWN_ROWS)
            acc = _dot(h_ref[r, rb * DOWN_ROWS:(rb + 1) * DOWN_ROWS, :], wd_ref[...])
            o_ref[blk, :] = _layer_norm(ALPHA * x_ref[blk, :] + acc, g_ref[...], b_ref[...])


def _ffn_ln(x, layer, wg, wu, wd, g, b, cargo=(), cargo_layer=0, tm=2 * FFN_ROWS):
    n = x.shape[0]
    steps = n // tm
    row = pl.BlockSpec((tm, D_MODEL), lambda i: (i, 0))
    c_in, c_out, c_shapes = _cargo_specs(cargo, cargo_layer, steps, lambda i: i)
    out = pl.pallas_call(
        functools.partial(_ffn_kernel, cargo_scales=tuple(sc for _, sc in cargo)),
        grid=(steps,),
        in_specs=[row, _resident_spec(wg), _resident_spec(wu), _resident_spec(wd), _resident_spec(g, layer),
                  _resident_spec(b, layer)] + c_in,
        out_specs=[row] + c_out,
        out_shape=[jax.ShapeDtypeStruct((n, D_MODEL), F32)] + c_shapes,
        scratch_shapes=[pltpu.VMEM((tm // FFN_ROWS, FFN_ROWS, D_FF), BF16)],
        compiler_params=_params("parallel"),
        name="ffn_ln",
    )(x, wg, wu, wd, g, b, *[a for a, _ in cargo])
    return out[0], out[1:]


def _cmul_add(ar, ai, xr, xi, br, bi):
    return ar * xr - ai * xi + br, ar * xi + ai * xr + bi


def _short_conv(bch, cw_ref, cb_ref, zbuf_ref):
    tt = bch.shape[0]
    z = bch[:, D_CONV:2 * D_CONV] * bch[:, 2 * D_CONV:3 * D_CONV]
    zbuf_ref[SUBLANES:SUBLANES + tt, :] = z
    z1 = zbuf_ref[SUBLANES - 1:SUBLANES - 1 + tt, :]
    z2 = zbuf_ref[SUBLANES - 2:SUBLANES - 2 + tt, :]
    y = cw_ref[0:1, :] * z2 + cw_ref[1:2, :] * z1 + cw_ref[2:3, :] * z + cb_ref[...]
    zbuf_ref[0:SUBLANES, :] = zbuf_ref[tt:tt + SUBLANES, :]
    return (bch[:, 0:D_CONV] * y).astype(BF16)


def _s5_scan(u, bbd_ref, astep_ref, aseg_ref, apow_ref, cbd_ref, st_ref, uh_ref, up_ref, xl_ref, xs_ref, side_work):
    for h in range(D_SSM // LANES):
        uh_ref[h] = u[:, h * LANES:(h + 1) * LANES]
        for r in range(SSM_STEPS):
            up_ref[r * SUBLANES:(r + 1) * SUBLANES, h * LANES:(h + 1) * LANES] = (
                uh_ref[h, pl.ds(r, SUBLANES, stride=SSM_STEPS), :])
    upb = up_ref[...].astype(BF16)
    first_row = lax.broadcasted_iota(jnp.int32, (SUBLANES, LANES), 0) == 0
    for c in range(N_CHUNKS):
        cols = slice(c * CHUNK_COLS, (c + 1) * CHUNK_COLS)
        re = slice(c * CHUNK_COLS, c * CHUNK_COLS + LANES)
        im = slice(c * CHUNK_COLS + LANES, (c + 1) * CHUNK_COLS)
        bu = _dot(upb, bbd_ref[:, cols])
        side_work(c)
        ar, ai = astep_ref[:, re], astep_ref[:, im]
        xr, xi = bu[0:SUBLANES, 0:LANES], bu[0:SUBLANES, LANES:CHUNK_COLS]
        xl_ref[0:SUBLANES, re] = xr
        xl_ref[0:SUBLANES, im] = xi
        for r in range(1, SSM_STEPS):
            rows = slice(r * SUBLANES, (r + 1) * SUBLANES)
            xr, xi = _cmul_add(ar, ai, xr, xi, bu[rows, 0:LANES], bu[rows, LANES:CHUNK_COLS])
            xl_ref[rows, re] = xr
            xl_ref[rows, im] = xi
        sr = jnp.where(first_row, st_ref[:, re], pltpu.roll(xr, 1, 0))
        si = jnp.where(first_row, st_ref[:, im], pltpu.roll(xi, 1, 0))
        for s, shift in enumerate((1, 2, 4)):
            rows = slice(s * SUBLANES, (s + 1) * SUBLANES)
            sr, si = _cmul_add(aseg_ref[rows, re], aseg_ref[rows, im],
                               pltpu.roll(sr, shift, 0), pltpu.roll(si, shift, 0), sr, si)
        rows = slice(3 * SUBLANES, 4 * SUBLANES)
        nr, ni = _cmul_add(aseg_ref[rows, re], aseg_ref[rows, im], sr, si, xr, xi)
        st_ref[:, re] = pltpu.roll(nr, 1, 0)
        st_ref[:, im] = pltpu.roll(ni, 1, 0)
        sr2 = jnp.concatenate([sr, sr], axis=0)
        si2 = jnp.concatenate([si, si], axis=0)
        for k in range(SSM_STEPS // 2):
            rows = slice(2 * k * SUBLANES, 2 * (k + 1) * SUBLANES)
            pr, pi = (jnp.concatenate([jnp.broadcast_to(apow_ref[2 * c + part, 2 * k + i:2 * k + i + 1, :],
                                                        (SUBLANES, LANES)) for i in range(2)], axis=0)
                      for part in range(2))
            fr, fi = _cmul_add(pr, pi, sr2, si2, xl_ref[rows, re], xl_ref[rows, im])
            xs_ref[rows, re] = fr.astype(BF16)
            xs_ref[rows, im] = fi.astype(BF16)
    return _dot(xs_ref[...], cbd_ref[...])


def _s5_finish(yp, u, d_ref, wglu_ref, yh_ref):
    halves = D_SSM // LANES
    for h in range(halves):
        for r in range(SSM_STEPS):
            yh_ref[h, pl.ds(r, SUBLANES, stride=SSM_STEPS), :] = (
                yp[r * SUBLANES:(r + 1) * SUBLANES, h * LANES:(h + 1) * LANES])
    y = jnp.concatenate([yh_ref[h] for h in range(halves)], axis=-1) + d_ref[...] * u
    y = jax.nn.gelu(y)
    y = y * jax.nn.sigmoid(_dot(y.astype(BF16), wglu_ref[...]))
    return y.astype(BF16)


def _swa_heads(q, k_all, v_all, mask, h, sink_ref, layer):
    dn = (((1,), (1,)), ((), ()))
    hs = slice(h * HEAD_DIM, (h + 1) * HEAD_DIM)
    heads = range(h * Q_PER_KV, (h + 1) * Q_PER_KV)
    qh = jnp.concatenate([q[:, j * HEAD_DIM:(j + 1) * HEAD_DIM] for j in heads], axis=0)
    s = lax.dot_general(qh, k_all[:, hs], dn, preferred_element_type=F32)
    probs, denoms = [], []
    for g, j in enumerate(heads):
        sink = sink_ref[layer, j]
        sg = jnp.where(mask, s[g * WINDOW:(g + 1) * WINDOW], -jnp.inf)
        m = jnp.maximum(jnp.max(sg, axis=-1, keepdims=True), sink)
        p = jnp.exp(sg - m)
        denoms.append(jnp.sum(p, axis=-1, keepdims=True) + jnp.exp(sink - m))
        probs.append(p.astype(BF16))
    o = _dot(jnp.concatenate(probs, axis=0), v_all[:, hs])
    return jnp.concatenate([o[g * WINDOW:(g + 1) * WINDOW] / denoms[g] for g in range(Q_PER_KV)],
                           axis=-1).astype(BF16)


def _mixer_kernel(sink_ref, x_ref, win_ref, wglu_ref, wc_ref, ws_ref, wa_ref, wo_ref, cw_ref, cb_ref, bbd_ref,
                  astep_ref, aseg_ref, apow_ref, cbd_ref, d_ref, g_ref, b_ref, *rest, layer, cargo_scales):
    n_cargo = len(cargo_scales)
    o_ref = rest[n_cargo]
    zbuf_ref, st_ref, uh_ref, up_ref, xl_ref, xs_ref, yh_ref, kv_ref, gates_ref = rest[2 * n_cargo + 1:]
    first_tile = pl.program_id(1) == 0

    @pl.when(first_tile)
    def _():
        zbuf_ref[0:SUBLANES, :] = jnp.zeros((SUBLANES, D_CONV), F32)
        st_ref[...] = jnp.zeros(st_ref.shape, F32)
        kv_ref[...] = jnp.zeros(kv_ref.shape, BF16)

    row = lax.broadcasted_iota(jnp.int32, (WINDOW, 2 * WINDOW), 0)
    col = lax.broadcasted_iota(jnp.int32, (WINDOW, 2 * WINDOW), 1)
    band = jnp.logical_and(col > row, col - WINDOW <= row)
    first_band = jnp.logical_and(band, jnp.logical_or(col >= WINDOW, jnp.logical_not(first_tile)))

    tt = SSM_TILE
    for sub in range(x_ref.shape[0] // tt):
        tok = slice(sub * tt, (sub + 1) * tt)
        x = x_ref[tok, :]
        xb = x.astype(BF16)

        def proj(lo, hi):
            return _dot(xb, win_ref[:, lo:hi])

        def gate_block(i):
            cols = slice(i * GATE_COLS, (i + 1) * GATE_COLS)
            gates_ref[sub, :, cols] = jax.nn.sigmoid(proj(OFF_G + i * GATE_COLS, OFF_G + (i + 1) * GATE_COLS))

        pending = list(range(N_BRANCH * D_MODEL // GATE_COLS))

        def gates(n):
            for _ in range(min(n, len(pending))):
                gate_block(pending.pop(0))

        u = proj(OFF_U, OFF_Q)
        y_conv = _short_conv(proj(0, OFF_U), cw_ref, cb_ref, zbuf_ref)
        q = (proj(OFF_Q, OFF_K) * (HEAD_DIM ** -0.5)).astype(BF16)
        k = jnp.concatenate([kv_ref[0], proj(OFF_K, OFF_V).astype(BF16)], axis=0)
        v = jnp.concatenate([kv_ref[1], proj(OFF_V, OFF_G).astype(BF16)], axis=0)
        kv_ref[0] = k[tt:tt + WINDOW]
        kv_ref[1] = v[tt:tt + WINDOW]
        attn = {}

        def attention_unit(i):
            j, h = divmod(i, N_KV_HEADS)
            attn[i] = _swa_heads(q[j * WINDOW:(j + 1) * WINDOW], k[j * WINDOW:(j + 2) * WINDOW],
                                 v[j * WINDOW:(j + 2) * WINDOW], band if (sub or j) else first_band, h,
                                 sink_ref, layer)

        def side_work(c):
            if c % 2 == 0:
                attention_unit(c // 2)
            else:
                gates(GATES_PER_CHUNK)

        gates(GATES_AHEAD_OF_SCAN)
        yp = _s5_scan(u, bbd_ref, astep_ref, aseg_ref, apow_ref, cbd_ref, st_ref, uh_ref, up_ref, xl_ref, xs_ref,
                      side_work)
        y_attn = jnp.concatenate(
            [jnp.concatenate([attn[j * N_KV_HEADS + h] for h in range(N_KV_HEADS)], axis=-1)
             for j in range(tt // WINDOW)], axis=0)
        gates(len(pending))

        y_ssm = _s5_finish(yp, u, d_ref, wglu_ref, yh_ref)

        if sub == 0:
            _narrow_cargo(rest[:n_cargo], rest[n_cargo + 1:2 * n_cargo + 1], cargo_scales)

        t_conv = _dot(y_conv, wc_ref[...])
        t_attn = _dot(y_attn, wa_ref[...])
        t_ssm = _dot(y_ssm, ws_ref[...])
        merged = (gates_ref[sub, :, 0:D_MODEL] * t_conv + gates_ref[sub, :, D_MODEL:2 * D_MODEL] * t_ssm
                  + gates_ref[sub, :, 2 * D_MODEL:3 * D_MODEL] * t_attn)
        mix = _dot(merged.astype(BF16), wo_ref[...])
        o_ref[tok, :] = _layer_norm(ALPHA * x + mix, g_ref[...], b_ref[...])


def _mixer(x, layer, batch, sinks, narrow_w, tables, cargo=()):
    n = x.shape[0]
    tt = SSM_TILE
    tile = MIX_SUBTILES * tt
    per_b = n // batch // tile
    row = pl.BlockSpec((tile, D_MODEL), lambda b, t: (b * per_b + t, 0))
    c_in, c_out, c_shapes = _cargo_specs(cargo, layer, batch * per_b, lambda b, t: b * per_b + t)
    out = pl.pallas_call(
        functools.partial(_mixer_kernel, layer=layer, cargo_scales=tuple(sc for _, sc in cargo)),
        grid=(batch, per_b),
        in_specs=[pl.BlockSpec(memory_space=pltpu.SMEM), row] + [_resident_spec(w) for w in narrow_w]
        + [_resident_spec(w, layer) for w in tables] + c_in,
        out_specs=[row] + c_out,
        out_shape=[jax.ShapeDtypeStruct((n, D_MODEL), F32)] + c_shapes,
        scratch_shapes=[pltpu.VMEM((tt + SUBLANES, D_CONV), F32),
                        pltpu.VMEM((SUBLANES, 2 * N_STATE), F32),
                        pltpu.VMEM((D_SSM // LANES, tt, LANES), F32),
                        pltpu.VMEM((tt, D_SSM), F32),
                        pltpu.VMEM((tt, 2 * N_STATE), F32),
                        pltpu.VMEM((tt, 2 * N_STATE), BF16),
                        pltpu.VMEM((D_SSM // LANES, tt, LANES), F32),
                        pltpu.VMEM((2, WINDOW, D_KV), BF16),
                        pltpu.VMEM((MIX_SUBTILES, tt, N_BRANCH * D_MODEL), F32)],
        compiler_params=_params("arbitrary", "arbitrary"),
        name="mixer",
    )(sinks, x, *narrow_w, *tables, *[a for a, _ in cargo])
    return out[0], out[1:]


def _ssm_tables(a_re, a_im, log_dt, b_re, b_im, c_re, c_im):
    dt = jnp.exp(log_dt)[..., None]
    mag = jnp.exp(a_re * dt)
    ang = a_im * dt
    abar_re = mag * jnp.cos(ang)
    abar_im = mag * jnp.sin(ang)
    nr = abar_re - 1.0
    ni = abar_im
    den = a_re * a_re + a_im * a_im
    coef_re = (nr * a_re + ni * a_im) / den
    coef_im = (ni * a_re - nr * a_im) / den
    bbar_re = coef_re[..., None] * b_re - coef_im[..., None] * b_im
    bbar_im = coef_re[..., None] * b_im + coef_im[..., None] * b_re

    group = jnp.arange(N_SSM_GROUPS)[:, None, None]
    sel = (group == GROUPS_PER_CHUNK * jnp.arange(N_CHUNKS)[None, :, None]
           + jnp.arange(GROUPS_PER_CHUNK)[None, None, :]).astype(F32)
    bbd = jnp.einsum('lagpc,gkh->lgckahp', jnp.stack([bbar_re, bbar_im], axis=1), sel)
    bbd = bbd.reshape(DEPTH, D_SSM, 2 * N_STATE).astype(BF16)
    cbd = jnp.einsum('lagcp,gkh->lkahpgc', jnp.stack([c_re, -c_im], axis=1), sel)
    cbd = cbd.reshape(DEPTH, 2 * N_STATE, D_SSM).astype(BF16)

    pr, pi = abar_re.reshape(DEPTH, 1, N_STATE), abar_im.reshape(DEPTH, 1, N_STATE)
    while pr.shape[1] < SSM_STEPS:
        tr, ti = pr[:, -1:], pi[:, -1:]
        pr, pi = (jnp.concatenate([pr, pr * tr - pi * ti], axis=1),
                  jnp.concatenate([pi, pr * ti + pi * tr], axis=1))

    def by_chunk(re, im):
        rows = re.shape[1]
        return jnp.stack([re.reshape(DEPTH, rows, N_CHUNKS, LANES), im.reshape(DEPTH, rows, N_CHUNKS, LANES)], axis=3)

    powers = by_chunk(pr, pi)
    apow = powers.transpose(0, 2, 3, 1, 4).reshape(DEPTH, 2 * N_CHUNKS, SSM_STEPS, LANES)
    astep = jnp.broadcast_to(powers[:, 0:1], (DEPTH, SUBLANES) + powers.shape[2:]).reshape(DEPTH, SUBLANES, 2 * N_STATE)

    sublane = jnp.arange(SUBLANES)[None, :, None]
    qr, qi = pr[:, -1:], pi[:, -1:]
    seg_re, seg_im = [], []
    for shift in (1, 2, 4):
        seg_re.append(jnp.where(sublane >= shift, qr, 0.0))
        seg_im.append(jnp.where(sublane >= shift, qi, 0.0))
        qr, qi = qr * qr - qi * qi, 2.0 * qr * qi
    seg_re.append(jnp.broadcast_to(pr[:, -1:], (DEPTH, SUBLANES, N_STATE)))
    seg_im.append(jnp.broadcast_to(pi[:, -1:], (DEPTH, SUBLANES, N_STATE)))
    aseg = by_chunk(jnp.concatenate(seg_re, axis=1), jnp.concatenate(seg_im, axis=1))
    return bbd, astep, aseg.reshape(DEPTH, 4 * SUBLANES, 2 * N_STATE), apow, cbd


def _vec(p):
    return p.reshape(DEPTH, 1, -1)


MIXER_NARROW = ('w_in', 'ssm_w_glu', 'w_br_conv', 'w_br_ssm', 'w_br_attn', 'w_out')


def _mixer_tables(p):
    bbd, astep, aseg, apow, cbd = _ssm_tables(p['ssm_a_re'], p['ssm_a_im'], p['ssm_log_dt'], p['ssm_b_re'],
                                              p['ssm_b_im'], p['ssm_c_re'], p['ssm_c_im'])
    return (p['conv_w'], _vec(p['conv_b']), bbd, astep, aseg, apow, cbd, _vec(p['ssm_d']),
            _vec(p['ln2_g']), _vec(p['ln2_b']))


def kernel(x, ffn1_w_gate, ffn1_w_up, ffn1_w_down, ln1_g, ln1_b, w_in, conv_w, conv_b, ssm_a_re, ssm_a_im, ssm_log_dt, ssm_b_re, ssm_b_im, ssm_c_re, ssm_c_im, ssm_d, ssm_w_glu, attn_sinks, w_br_conv, w_br_ssm, w_br_attn, w_out, ln2_g, ln2_b, ffn2_w_gate, ffn2_w_up, ffn2_w_down, ln3_g, ln3_b):
    batch, seq, _ = x.shape
    h = x.reshape(batch * seq, D_MODEL)
    p = dict(w_in=w_in, conv_w=conv_w, conv_b=conv_b, ssm_a_re=ssm_a_re, ssm_a_im=ssm_a_im, ssm_log_dt=ssm_log_dt,
             ssm_b_re=ssm_b_re, ssm_b_im=ssm_b_im, ssm_c_re=ssm_c_re, ssm_c_im=ssm_c_im, ssm_d=ssm_d,
             ssm_w_glu=ssm_w_glu, w_br_conv=w_br_conv, w_br_ssm=w_br_ssm, w_br_attn=w_br_attn, w_out=w_out,
             ln2_g=ln2_g, ln2_b=ln2_b)
    tables = _mixer_tables(p)
    mixer_cargo = [(p[name], 1.0) for name in MIXER_NARROW]
    ffn1_cargo = [(ffn1_w_gate, 1.0), (ffn1_w_up, 1.0), (ffn1_w_down, 0.5)]
    ffn2_cargo = [(ffn2_w_gate, 1.0), (ffn2_w_up, 1.0), (ffn2_w_down, 0.5)]
    ffn_w = (ffn1_w_gate[0].astype(BF16), ffn1_w_up[0].astype(BF16), (0.5 * ffn1_w_down[0]).astype(BF16))
    ln1, ln3 = (_vec(ln1_g), _vec(ln1_b)), (_vec(ln3_g), _vec(ln3_b))
    for l in range(DEPTH):
        h, mix_w = _ffn_ln(h, l, *ffn_w, *ln1, cargo=mixer_cargo, cargo_layer=l)
        h, ffn_w = _mixer(h, l, batch, attn_sinks, mix_w, tables, cargo=ffn2_cargo)
        last = l == DEPTH - 1
        h, ffn_w = _ffn_ln(h, l, *ffn_w, *ln3, cargo=() if last else ffn1_cargo, cargo_layer=l + 1)
    return h.reshape(batch, seq, D_MODEL)
```

```python
import functools

import jax
import jax.numpy as jnp
from jax import lax
from jax.experimental import pallas as pl
from jax.experimental.pallas import tpu as pltpu

D_MODEL = 1024
DEPTH = 4
D_CONV = 256
CONV_WIDTH = 3
D_SSM = 256
SSM_GROUP = 16
N_SSM_GROUPS = D_SSM // SSM_GROUP
SSM_STATE = 64
N_STATE = N_SSM_GROUPS * SSM_STATE
N_Q_HEADS = 8
N_KV_HEADS = 2
Q_PER_KV = N_Q_HEADS // N_KV_HEADS
HEAD_DIM = 64
D_ATTN = N_Q_HEADS * HEAD_DIM
D_KV = N_KV_HEADS * HEAD_DIM
WINDOW = 128
N_BRANCH = 3
D_FF = 2816
ALPHA = (2 * DEPTH) ** 0.25
LN_EPS = 1e-5
D_IN = 3 * D_CONV + D_SSM + D_ATTN + 2 * D_KV + N_BRANCH * D_MODEL

OFF_U = 3 * D_CONV
OFF_Q = OFF_U + D_SSM
OFF_K = OFF_Q + D_ATTN
OFF_V = OFF_K + D_KV
OFF_G = OFF_V + D_KV

SUBLANES = 8
LANES = 128
BF16_ROWS = 16
FF_CHUNK = 256
FFN_ROWS = 512
DOWN_ROWS = 256
VMEM_LIMIT = 56 * 1024 * 1024

SSM_TILE = 256
SSM_STEPS = SSM_TILE // SUBLANES
N_CHUNKS = N_STATE // LANES
CHUNK_COLS = 2 * LANES
GROUPS_PER_CHUNK = LANES // SSM_STATE
GATE_COLS = 256
MIX_SUBTILES = 2
GATES_AHEAD_OF_SCAN = 2
GATES_PER_CHUNK = 1

F32 = jnp.float32
BF16 = jnp.bfloat16


def _dot(a, b):
    return jnp.dot(a, b, preferred_element_type=F32)


def _layer_norm(r, g, b):
    mu = jnp.mean(r, axis=-1, keepdims=True)
    d = r - mu
    var = jnp.mean(d * d, axis=-1, keepdims=True)
    return d * lax.rsqrt(var + LN_EPS) * g + b


def _resident_spec(arr, layer=None):
    if layer is None:
        return pl.BlockSpec(arr.shape, lambda *_: (0,) * arr.ndim, pipeline_mode=pl.Buffered(1))
    zeros = (0,) * (arr.ndim - 1)
    return pl.BlockSpec((None,) + arr.shape[1:], lambda *_: (layer,) + zeros, pipeline_mode=pl.Buffered(1))


def _cargo_specs(cargo, layer, steps, step_of):
    in_specs, out_specs, out_shapes = [], [], []
    for arr, _ in cargo:
        _, rows, cols = arr.shape
        blocks = steps if rows % (steps * BF16_ROWS) == 0 else steps // 2
        assert rows % (blocks * BF16_ROWS) == 0 and steps % blocks == 0, (arr.shape, steps)
        per = steps // blocks
        in_specs.append(pl.BlockSpec((None, rows // blocks, cols),
                                     lambda *ids, per=per: (layer, step_of(*ids) // per, 0)))
        out_specs.append(pl.BlockSpec((rows // blocks, cols), lambda *ids, per=per: (step_of(*ids) // per, 0)))
        out_shapes.append(jax.ShapeDtypeStruct((rows, cols), BF16))
    return in_specs, out_specs, out_shapes


def _narrow_cargo(in_refs, out_refs, scales):
    for src, dst, scale in zip(in_refs, out_refs, scales):
        w = src[...]
        dst[...] = (w if scale == 1.0 else w * scale).astype(BF16)


def _params(*sem):
    return pltpu.CompilerParams(dimension_semantics=sem, vmem_limit_bytes=VMEM_LIMIT)


def _ffn_kernel(x_ref, wg_ref, wu_ref, wd_ref, g_ref, b_ref, *rest, cargo_scales):
    n_cargo = len(cargo_scales)
    o_ref = rest[n_cargo]
    h_ref = rest[-1]
    n_chunks = D_FF // FF_CHUNK
    xb_all = x_ref[...].astype(BF16)
    for r in range(x_ref.shape[0] // FFN_ROWS):
        rows = slice(r * FFN_ROWS, (r + 1) * FFN_ROWS)
        xb = xb_all[rows, :]
        for c in range(n_chunks):
            cols = slice(c * FF_CHUNK, (c + 1) * FF_CHUNK)
            gate = _dot(xb, wg_ref[:, cols])
            up = _dot(xb, wu_ref[:, cols])
            h_ref[r, :, cols] = (jax.nn.silu(gate) * up).astype(BF16)
            if r == 0 and c == n_chunks // 2:
                _narrow_cargo(rest[:n_cargo], rest[n_cargo + 1:2 * n_cargo + 1], cargo_scales)
        for rb in range(FFN_ROWS // DOWN_ROWS):
            blk = slice(r * FFN_ROWS + rb * DOWN_ROWS, r * FFN_ROWS + (rb + 1) * DOWN_ROWS)
            acc = _dot(h_ref[r, rb * DOWN_ROWS:(rb + 1) * DOWN_ROWS, :], wd_ref[...])
            o_ref[blk, :] = _layer_norm(ALPHA * x_ref[blk, :] + acc, g_ref[...], b_ref[...])


def _ffn_ln(x, layer, wg, wu, wd, g, b, cargo=(), cargo_layer=0, tm=2 * FFN_ROWS):
    n = x.shape[0]
    steps = n // tm
    row = pl.BlockSpec((tm, D_MODEL), lambda i: (i, 0))
    c_in, c_out, c_shapes = _cargo_specs(cargo, cargo_layer, steps, lambda i: i)
    out = pl.pallas_call(
        functools.partial(_ffn_kernel, cargo_scales=tuple(sc for _, sc in cargo)),
        grid=(steps,),
        in_specs=[row, _resident_spec(wg), _resident_spec(wu), _resident_spec(wd), _resident_spec(g, layer),
                  _resident_spec(b, layer)] + c_in,
        out_specs=[row] + c_out,
        out_shape=[jax.ShapeDtypeStruct((n, D_MODEL), F32)] + c_shapes,
        scratch_shapes=[pltpu.VMEM((tm // FFN_ROWS, FFN_ROWS, D_FF), BF16)],
        compiler_params=_params("parallel"),
        name="ffn_ln",
    )(x, wg, wu, wd, g, b, *[a for a, _ in cargo])
    return out[0], out[1:]


def _cmul_add(ar, ai, xr, xi, br, bi):
    return ar * xr - ai * xi + br, ar * xi + ai * xr + bi


def _short_conv(bch, cw_ref, cb_ref, zbuf_ref):
    tt = bch.shape[0]
    z = bch[:, D_CONV:2 * D_CONV] * bch[:, 2 * D_CONV:3 * D_CONV]
    zbuf_ref[SUBLANES:SUBLANES + tt, :] = z
    z1 = zbuf_ref[SUBLANES - 1:SUBLANES - 1 + tt, :]
    z2 = zbuf_ref[SUBLANES - 2:SUBLANES - 2 + tt, :]
    y = cw_ref[0:1, :] * z2 + cw_ref[1:2, :] * z1 + cw_ref[2:3, :] * z + cb_ref[...]
    zbuf_ref[0:SUBLANES, :] = zbuf_ref[tt:tt + SUBLANES, :]
    return (bch[:, 0:D_CONV] * y).astype(BF16)


def _s5_scan(u, bbd_ref, astep_ref, aseg_ref, apow_ref, cbd_ref, st_ref, uh_ref, up_ref, xl_ref, xs_ref, side_work):
    for h in range(D_SSM // LANES):
        uh_ref[h] = u[:, h * LANES:(h + 1) * LANES]
        for r in range(SSM_STEPS):
            up_ref[r * SUBLANES:(r + 1) * SUBLANES, h * LANES:(h + 1) * LANES] = (
                uh_ref[h, pl.ds(r, SUBLANES, stride=SSM_STEPS), :])
    upb = up_ref[...].astype(BF16)
    first_row = lax.broadcasted_iota(jnp.int32, (SUBLANES, LANES), 0) == 0
    for c in range(N_CHUNKS):
        re = slice(c * CHUNK_COLS, c * CHUNK_COLS + LANES)
        im = slice(c * CHUNK_COLS + LANES, (c + 1) * CHUNK_COLS)
        bu = _dot(upb, jnp.concatenate([bbd_ref[2 * c], bbd_ref[2 * c + 1]], axis=1))
        side_work(c)
        ar, ai = astep_ref[:, re], astep_ref[:, im]
        xr, xi = bu[0:SUBLANES, 0:LANES], bu[0:SUBLANES, LANES:CHUNK_COLS]
        xl_ref[0:SUBLANES, re] = xr
        xl_ref[0:SUBLANES, im] = xi
        for r in range(1, SSM_STEPS):
            rows = slice(r * SUBLANES, (r + 1) * SUBLANES)
            xr, xi = _cmul_add(ar, ai, xr, xi, bu[rows, 0:LANES], bu[rows, LANES:CHUNK_COLS])
            xl_ref[rows, re] = xr
            xl_ref[rows, im] = xi
        sr = jnp.where(first_row, st_ref[:, re], pltpu.roll(xr, 1, 0))
        si = jnp.where(first_row, st_ref[:, im], pltpu.roll(xi, 1, 0))
        for s, shift in enumerate((1, 2, 4)):
            rows = slice(s * SUBLANES, (s + 1) * SUBLANES)
            sr, si = _cmul_add(aseg_ref[rows, re], aseg_ref[rows, im],
                               pltpu.roll(sr, shift, 0), pltpu.roll(si, shift, 0), sr, si)
        rows = slice(3 * SUBLANES, 4 * SUBLANES)
        nr, ni = _cmul_add(aseg_ref[rows, re], aseg_ref[rows, im], sr, si, xr, xi)
        st_ref[:, re] = pltpu.roll(nr, 1, 0)
        st_ref[:, im] = pltpu.roll(ni, 1, 0)
        sr2 = jnp.concatenate([sr, sr], axis=0)
        si2 = jnp.concatenate([si, si], axis=0)
        for k in range(SSM_STEPS // 2):
            rows = slice(2 * k * SUBLANES, 2 * (k + 1) * SUBLANES)
            pr, pi = (jnp.concatenate([jnp.broadcast_to(apow_ref[2 * c + part, 2 * k + i:2 * k + i + 1, :],
                                                        (SUBLANES, LANES)) for i in range(2)], axis=0)
                      for part in range(2))
            fr, fi = _cmul_add(pr, pi, sr2, si2, xl_ref[rows, re], xl_ref[rows, im])
            xs_ref[rows, re] = fr.astype(BF16)
            xs_ref[rows, im] = fi.astype(BF16)
    return _dot(xs_ref[...], cbd_ref[...])


def _s5_finish(yp, u, d_ref, wglu_ref, yh_ref):
    halves = D_SSM // LANES
    for h in range(halves):
        for r in range(SSM_STEPS):
            yh_ref[h, pl.ds(r, SUBLANES, stride=SSM_STEPS), :] = (
                yp[r * SUBLANES:(r + 1) * SUBLANES, h * LANES:(h + 1) * LANES])
    y = jnp.concatenate([yh_ref[h] for h in range(halves)], axis=-1) + d_ref[...] * u
    y = jax.nn.gelu(y)
    y = y * jax.nn.sigmoid(_dot(y.astype(BF16), wglu_ref[...]))
    return y.astype(BF16)


def _swa_heads(q, k_all, v_all, mask, h, sink_ref, layer):
    dn = (((1,), (1,)), ((), ()))
    hs = slice(h * HEAD_DIM, (h + 1) * HEAD_DIM)
    heads = range(h * Q_PER_KV, (h + 1) * Q_PER_KV)
    qh = jnp.concatenate([q[:, j * HEAD_DIM:(j + 1) * HEAD_DIM] for j in heads], axis=0)
    s = lax.dot_general(qh, k_all[:, hs], dn, preferred_element_type=F32)
    probs, denoms = [], []
    for g, j in enumerate(heads):
        sink = sink_ref[layer, j]
        sg = jnp.where(mask, s[g * WINDOW:(g + 1) * WINDOW], -jnp.inf)
        m = jnp.maximum(jnp.max(sg, axis=-1, keepdims=True), sink)
        p = jnp.exp(sg - m)
        denoms.append(jnp.sum(p, axis=-1, keepdims=True) + jnp.exp(sink - m))
        probs.append(p.astype(BF16))
    o = _dot(jnp.concatenate(probs, axis=0), v_all[:, hs])
    return jnp.concatenate([o[g * WINDOW:(g + 1) * WINDOW] / denoms[g] for g in range(Q_PER_KV)],
                           axis=-1).astype(BF16)


def _mixer_kernel(sink_ref, x_ref, win_ref, wglu_ref, wc_ref, ws_ref, wa_ref, wo_ref, cw_ref, cb_ref, bbd_ref,
                  astep_ref, aseg_ref, apow_ref, cbd_ref, d_ref, g_ref, b_ref, *rest, layer, cargo_scales):
    n_cargo = len(cargo_scales)
    o_ref = rest[n_cargo]
    zbuf_ref, st_ref, uh_ref, up_ref, xl_ref, xs_ref, yh_ref, kv_ref, gates_ref = rest[2 * n_cargo + 1:]
    first_tile = pl.program_id(1) == 0

    @pl.when(first_tile)
    def _():
        zbuf_ref[0:SUBLANES, :] = jnp.zeros((SUBLANES, D_CONV), F32)
        st_ref[...] = jnp.zeros(st_ref.shape, F32)
        kv_ref[...] = jnp.zeros(kv_ref.shape, BF16)

    row = lax.broadcasted_iota(jnp.int32, (WINDOW, 2 * WINDOW), 0)
    col = lax.broadcasted_iota(jnp.int32, (WINDOW, 2 * WINDOW), 1)
    band = jnp.logical_and(col > row, col - WINDOW <= row)
    first_band = jnp.logical_and(band, jnp.logical_or(col >= WINDOW, jnp.logical_not(first_tile)))

    tt = SSM_TILE
    for sub in range(x_ref.shape[0] // tt):
        tok = slice(sub * tt, (sub + 1) * tt)
        x = x_ref[tok, :]
        xb = x.astype(BF16)

        def proj(lo, hi):
            return _dot(xb, win_ref[:, lo:hi])

        def gate_block(i):
            cols = slice(i * GATE_COLS, (i + 1) * GATE_COLS)
            gates_ref[sub, :, cols] = jax.nn.sigmoid(proj(OFF_G + i * GATE_COLS, OFF_G + (i + 1) * GATE_COLS))

        pending = list(range(N_BRANCH * D_MODEL // GATE_COLS))

        def gates(n):
            for _ in range(min(n, len(pending))):
                gate_block(pending.pop(0))

        u = proj(OFF_U, OFF_Q)
        y_conv = _short_conv(proj(0, OFF_U), cw_ref, cb_ref, zbuf_ref)
        q = (proj(OFF_Q, OFF_K) * (HEAD_DIM ** -0.5)).astype(BF16)
        k = jnp.concatenate([kv_ref[0], proj(OFF_K, OFF_V).astype(BF16)], axis=0)
        v = jnp.concatenate([kv_ref[1], proj(OFF_V, OFF_G).astype(BF16)], axis=0)
        kv_ref[0] = k[tt:tt + WINDOW]
        kv_ref[1] = v[tt:tt + WINDOW]
        attn = {}

        def attention_unit(i):
            j, h = divmod(i, N_KV_HEADS)
            attn[i] = _swa_heads(q[j * WINDOW:(j + 1) * WINDOW], k[j * WINDOW:(j + 2) * WINDOW],
                                 v[j * WINDOW:(j + 2) * WINDOW], band if (sub or j) else first_band, h,
                                 sink_ref, layer)

        def side_work(c):
            if c % 2 == 0:
                attention_unit(c // 2)
            else:
                gates(GATES_PER_CHUNK)

        gates(GATES_AHEAD_OF_SCAN)
        yp = _s5_scan(u, bbd_ref, astep_ref, aseg_ref, apow_ref, cbd_ref, st_ref, uh_ref, up_ref, xl_ref, xs_ref,
                      side_work)
        y_attn = jnp.concatenate(
            [jnp.concatenate([attn[j * N_KV_HEADS + h] for h in range(N_KV_HEADS)], axis=-1)
             for j in range(tt // WINDOW)], axis=0)
        gates(len(pending))

        y_ssm = _s5_finish(yp, u, d_ref, wglu_ref, yh_ref)

        if sub == 0:
            _narrow_cargo(rest[:n_cargo], rest[n_cargo + 1:2 * n_cargo + 1], cargo_scales)

        t_conv = _dot(y_conv, wc_ref[...])
        t_attn = _dot(y_attn, wa_ref[...])
        t_ssm = _dot(y_ssm, ws_ref[...])
        merged = (gates_ref[sub, :, 0:D_MODEL] * t_conv + gates_ref[sub, :, D_MODEL:2 * D_MODEL] * t_ssm
                  + gates_ref[sub, :, 2 * D_MODEL:3 * D_MODEL] * t_attn)
        mix = _dot(merged.astype(BF16), wo_ref[...])
        o_ref[tok, :] = _layer_norm(ALPHA * x + mix, g_ref[...], b_ref[...])


def _mixer(x, layer, batch, sinks, narrow_w, tables, cargo=()):
    n = x.shape[0]
    tt = SSM_TILE
    tile = MIX_SUBTILES * tt
    per_b = n // batch // tile
    row = pl.BlockSpec((tile, D_MODEL), lambda b, t: (b * per_b + t, 0))
    c_in, c_out, c_shapes = _cargo_specs(cargo, layer, batch * per_b, lambda b, t: b * per_b + t)
    out = pl.pallas_call(
        functools.partial(_mixer_kernel, layer=layer, cargo_scales=tuple(sc for _, sc in cargo)),
        grid=(batch, per_b),
        in_specs=[pl.BlockSpec(memory_space=pltpu.SMEM), row] + [_resident_spec(w) for w in narrow_w]
        + [_resident_spec(w, layer) for w in tables] + c_in,
        out_specs=[row] + c_out,
        out_shape=[jax.ShapeDtypeStruct((n, D_MODEL), F32)] + c_shapes,
        scratch_shapes=[pltpu.VMEM((tt + SUBLANES, D_CONV), F32),
                        pltpu.VMEM((SUBLANES, 2 * N_STATE), F32),
                        pltpu.VMEM((D_SSM // LANES, tt, LANES), F32),
                        pltpu.VMEM((tt, D_SSM), F32),
                        pltpu.VMEM((tt, 2 * N_STATE), F32),
                        pltpu.VMEM((tt, 2 * N_STATE), BF16),
                        pltpu.VMEM((D_SSM // LANES, tt, LANES), F32),
                        pltpu.VMEM((2, WINDOW, D_KV), BF16),
                        pltpu.VMEM((MIX_SUBTILES, tt, N_BRANCH * D_MODEL), F32)],
        compiler_params=_params("arbitrary", "arbitrary"),
        name="mixer",
    )(sinks, x, *narrow_w, *tables, *[a for a, _ in cargo])
    return out[0], out[1:]


def _ssm_tables(a_re, a_im, log_dt, b_re, b_im, c_re, c_im):
    dt = jnp.exp(log_dt)[..., None]
    mag = jnp.exp(a_re * dt)
    ang = a_im * dt
    abar_re = mag * jnp.cos(ang)
    abar_im = mag * jnp.sin(ang)
    nr = abar_re - 1.0
    ni = abar_im
    den = a_re * a_re + a_im * a_im
    coef_re = (nr * a_re + ni * a_im) / den
    coef_im = (ni * a_re - nr * a_im) / den
    bbar_re = coef_re[..., None] * b_re - coef_im[..., None] * b_im
    bbar_im = coef_re[..., None] * b_im + coef_im[..., None] * b_re

    chunk = jnp.arange(N_CHUNKS)[:, None]
    lane_group = GROUPS_PER_CHUNK * chunk + jnp.arange(LANES)[None, :] // SSM_STATE
    chan_group = jnp.arange(D_SSM) // SSM_GROUP
    b_lane = jnp.stack([bbar_re, bbar_im], axis=1).transpose(0, 1, 2, 4, 3)
    b_lane = jnp.tile(b_lane.reshape(DEPTH, 2, D_SSM, SSM_STATE), (1, 1, 1, GROUPS_PER_CHUNK))
    sel_b = (chan_group[None, :, None] == lane_group[:, None, :]).astype(F32)
    bbd = (b_lane[:, None] * sel_b[None, :, None]).reshape(DEPTH, 2 * N_CHUNKS, D_SSM, LANES).astype(BF16)
    c_chan = jnp.stack([c_re, -c_im], axis=1).transpose(0, 1, 4, 2, 3).reshape(DEPTH, 2, SSM_STATE, D_SSM)
    c_chan = jnp.tile(c_chan, (1, 1, GROUPS_PER_CHUNK, 1))
    sel_c = (lane_group[:, :, None] == chan_group[None, None, :]).astype(F32)
    cbd = (c_chan[:, None] * sel_c[None, :, None]).reshape(DEPTH, 2 * N_STATE, D_SSM).astype(BF16)

    pr, pi = abar_re.reshape(DEPTH, 1, N_STATE), abar_im.reshape(DEPTH, 1, N_STATE)
    while pr.shape[1] < SSM_STEPS:
        tr, ti = pr[:, -1:], pi[:, -1:]
        pr, pi = (jnp.concatenate([pr, pr * tr - pi * ti], axis=1),
                  jnp.concatenate([pi, pr * ti + pi * tr], axis=1))

    def by_chunk(re, im):
        rows = re.shape[1]
        return jnp.stack([re.reshape(DEPTH, rows, N_CHUNKS, LANES), im.reshape(DEPTH, rows, N_CHUNKS, LANES)], axis=3)

    powers = by_chunk(pr, pi)
    apow = powers.transpose(0, 2, 3, 1, 4).reshape(DEPTH, 2 * N_CHUNKS, SSM_STEPS, LANES)
    astep = jnp.broadcast_to(powers[:, 0:1], (DEPTH, SUBLANES) + powers.shape[2:]).reshape(DEPTH, SUBLANES, 2 * N_STATE)

    sublane = jnp.arange(SUBLANES)[None, :, None]
    qr, qi = pr[:, -1:], pi[:, -1:]
    seg_re, seg_im = [], []
    for shift in (1, 2, 4):
        seg_re.append(jnp.where(sublane >= shift, qr, 0.0))
        seg_im.append(jnp.where(sublane >= shift, qi, 0.0))
        qr, qi = qr * qr - qi * qi, 2.0 * qr * qi
    seg_re.append(jnp.broadcast_to(pr[:, -1:], (DEPTH, SUBLANES, N_STATE)))
    seg_im.append(jnp.broadcast_to(pi[:, -1:], (DEPTH, SUBLANES, N_STATE)))
    aseg = by_chunk(jnp.concatenate(seg_re, axis=1), jnp.concatenate(seg_im, axis=1))
    return bbd, astep, aseg.reshape(DEPTH, 4 * SUBLANES, 2 * N_STATE), apow, cbd


def _vec(p):
    return p.reshape(DEPTH, 1, -1)


MIXER_NARROW = ('w_in', 'ssm_w_glu', 'w_br_conv', 'w_br_ssm', 'w_br_attn', 'w_out')


def _mixer_tables(p):
    bbd, astep, aseg, apow, cbd = _ssm_tables(p['ssm_a_re'], p['ssm_a_im'], p['ssm_log_dt'], p['ssm_b_re'],
                                              p['ssm_b_im'], p['ssm_c_re'], p['ssm_c_im'])
    return (p['conv_w'], _vec(p['conv_b']), bbd, astep, aseg, apow, cbd, _vec(p['ssm_d']),
            _vec(p['ln2_g']), _vec(p['ln2_b']))


def kernel(x, ffn1_w_gate, ffn1_w_up, ffn1_w_down, ln1_g, ln1_b, w_in, conv_w, conv_b, ssm_a_re, ssm_a_im, ssm_log_dt, ssm_b_re, ssm_b_im, ssm_c_re, ssm_c_im, ssm_d, ssm_w_glu, attn_sinks, w_br_conv, w_br_ssm, w_br_attn, w_out, ln2_g, ln2_b, ffn2_w_gate, ffn2_w_up, ffn2_w_down, ln3_g, ln3_b):
    batch, seq, _ = x.shape
    h = x.reshape(batch * seq, D_MODEL)
    p = dict(w_in=w_in, conv_w=conv_w, conv_b=conv_b, ssm_a_re=ssm_a_re, ssm_a_im=ssm_a_im, ssm_log_dt=ssm_log_dt,
             ssm_b_re=ssm_b_re, ssm_b_im=ssm_b_im, ssm_c_re=ssm_c_re, ssm_c_im=ssm_c_im, ssm_d=ssm_d,
             ssm_w_glu=ssm_w_glu, w_br_conv=w_br_conv, w_br_ssm=w_br_ssm, w_br_attn=w_br_attn, w_out=w_out,
             ln2_g=ln2_g, ln2_b=ln2_b)
    tables = _mixer_tables(p)
    mixer_cargo = [(p[name], 1.0) for name in MIXER_NARROW]
    ffn1_cargo = [(ffn1_w_gate, 1.0), (ffn1_w_up, 1.0), (ffn1_w_down, 0.5)]
    ffn2_cargo = [(ffn2_w_gate, 1.0), (ffn2_w_up, 1.0), (ffn2_w_down, 0.5)]
    ffn_w = (ffn1_w_gate[0].astype(BF16), ffn1_w_up[0].astype(BF16), (0.5 * ffn1_w_down[0]).astype(BF16))
    ln1, ln3 = (_vec(ln1_g), _vec(ln1_b)), (_vec(ln3_g), _vec(ln3_b))
    for l in range(DEPTH):
        h, mix_w = _ffn_ln(h, l, *ffn_w, *ln1, cargo=mixer_cargo, cargo_layer=l)
        h, ffn_w = _mixer(h, l, batch, attn_sinks, mix_w, tables, cargo=ffn2_cargo)
        last = l == DEPTH - 1
        h, ffn_w = _ffn_ln(h, l, *ffn_w, *ln3, cargo=() if last else ffn1_cargo, cargo_layer=l + 1)
    return h.reshape(batch, seq, D_MODEL)
```

```python
import functools

import jax
import jax.numpy as jnp
from jax import lax
from jax.experimental import pallas as pl
from jax.experimental.pallas import tpu as pltpu

D_MODEL = 1024
DEPTH = 4
D_CONV = 256
CONV_WIDTH = 3
D_SSM = 256
SSM_GROUP = 16
N_SSM_GROUPS = D_SSM // SSM_GROUP
SSM_STATE = 64
N_STATE = N_SSM_GROUPS * SSM_STATE
N_Q_HEADS = 8
N_KV_HEADS = 2
Q_PER_KV = N_Q_HEADS // N_KV_HEADS
HEAD_DIM = 64
D_ATTN = N_Q_HEADS * HEAD_DIM
D_KV = N_KV_HEADS * HEAD_DIM
WINDOW = 128
N_BRANCH = 3
D_FF = 2816
ALPHA = (2 * DEPTH) ** 0.25
LN_EPS = 1e-5
D_IN = 3 * D_CONV + D_SSM + D_ATTN + 2 * D_KV + N_BRANCH * D_MODEL

OFF_U = 3 * D_CONV
OFF_Q = OFF_U + D_SSM
OFF_K = OFF_Q + D_ATTN
OFF_V = OFF_K + D_KV
OFF_G = OFF_V + D_KV

SUBLANES = 8
LANES = 128
BF16_ROWS = 16
FF_CHUNK = 256
FFN_ROWS = 512
DOWN_ROWS = 256
VMEM_LIMIT = 56 * 1024 * 1024

SSM_TILE = 256
SSM_STEPS = SSM_TILE // SUBLANES
N_CHUNKS = N_STATE // LANES
CHUNK_COLS = 2 * LANES
GROUPS_PER_CHUNK = LANES // SSM_STATE
GATE_COLS = 256
MIX_SUBTILES = 2
GATES_AHEAD_OF_SCAN = 1
GATES_PER_CHUNK = 1

F32 = jnp.float32
BF16 = jnp.bfloat16


def _dot(a, b):
    return jnp.dot(a, b, preferred_element_type=F32)


def _layer_norm(r, g, b):
    mu = jnp.mean(r, axis=-1, keepdims=True)
    d = r - mu
    var = jnp.mean(d * d, axis=-1, keepdims=True)
    return d * lax.rsqrt(var + LN_EPS) * g + b


def _resident_spec(arr, layer=None):
    if layer is None:
        return pl.BlockSpec(arr.shape, lambda *_: (0,) * arr.ndim, pipeline_mode=pl.Buffered(1))
    zeros = (0,) * (arr.ndim - 1)
    return pl.BlockSpec((None,) + arr.shape[1:], lambda *_: (layer,) + zeros, pipeline_mode=pl.Buffered(1))


def _cargo_specs(cargo, layer, steps, step_of):
    in_specs, out_specs, out_shapes = [], [], []
    for arr, _ in cargo:
        _, rows, cols = arr.shape
        blocks = steps if rows % (steps * BF16_ROWS) == 0 else steps // 2
        assert rows % (blocks * BF16_ROWS) == 0 and steps % blocks == 0, (arr.shape, steps)
        per = steps // blocks
        in_specs.append(pl.BlockSpec((None, rows // blocks, cols),
                                     lambda *ids, per=per: (layer, step_of(*ids) // per, 0)))
        out_specs.append(pl.BlockSpec((rows // blocks, cols), lambda *ids, per=per: (step_of(*ids) // per, 0)))
        out_shapes.append(jax.ShapeDtypeStruct((rows, cols), BF16))
    return in_specs, out_specs, out_shapes


def _narrow_cargo(in_refs, out_refs, scales):
    for src, dst, scale in zip(in_refs, out_refs, scales):
        w = src[...]
        dst[...] = (w if scale == 1.0 else w * scale).astype(BF16)


def _params(*sem):
    return pltpu.CompilerParams(dimension_semantics=sem, vmem_limit_bytes=VMEM_LIMIT)


def _ffn_kernel(x_ref, wg_ref, wu_ref, wd_ref, g_ref, b_ref, *rest, cargo_scales):
    n_cargo = len(cargo_scales)
    o_ref = rest[n_cargo]
    h_ref = rest[-1]
    n_chunks = D_FF // FF_CHUNK
    xb_all = x_ref[...].astype(BF16)
    for r in range(x_ref.shape[0] // FFN_ROWS):
        rows = slice(r * FFN_ROWS, (r + 1) * FFN_ROWS)
        xb = xb_all[rows, :]
        for c in range(n_chunks):
            cols = slice(c * FF_CHUNK, (c + 1) * FF_CHUNK)
            gate = _dot(xb, wg_ref[:, cols])
            up = _dot(xb, wu_ref[:, cols])
            h_ref[r, :, cols] = (jax.nn.silu(gate) * up).astype(BF16)
            if r == 0 and c == n_chunks // 2:
                _narrow_cargo(rest[:n_cargo], rest[n_cargo + 1:2 * n_cargo + 1], cargo_scales)
        for rb in range(FFN_ROWS // DOWN_ROWS):
            blk = slice(r * FFN_ROWS + rb * DOWN_ROWS, r * FFN_ROWS + (rb + 1) * DOWN_ROWS)
            acc = _dot(h_ref[r, rb * DOWN_ROWS:(rb + 1) * DOWN_ROWS, :], wd_ref[...])
            o_ref[blk, :] = _layer_norm(ALPHA * x_ref[blk, :] + acc, g_ref[...], b_ref[...])


def _ffn_ln(x, layer, wg, wu, wd, g, b, cargo=(), cargo_layer=0, tm=2 * FFN_ROWS):
    n = x.shape[0]
    steps = n // tm
    row = pl.BlockSpec((tm, D_MODEL), lambda i: (i, 0))
    c_in, c_out, c_shapes = _cargo_specs(cargo, cargo_layer, steps, lambda i: i)
    out = pl.pallas_call(
        functools.partial(_ffn_kernel, cargo_scales=tuple(sc for _, sc in cargo)),
        grid=(steps,),
        in_specs=[row, _resident_spec(wg), _resident_spec(wu), _resident_spec(wd), _resident_spec(g, layer),
                  _resident_spec(b, layer)] + c_in,
        out_specs=[row] + c_out,
        out_shape=[jax.ShapeDtypeStruct((n, D_MODEL), F32)] + c_shapes,
        scratch_shapes=[pltpu.VMEM((tm // FFN_ROWS, FFN_ROWS, D_FF), BF16)],
        compiler_params=_params("parallel"),
        name="ffn_ln",
    )(x, wg, wu, wd, g, b, *[a for a, _ in cargo])
    return out[0], out[1:]


def _cmul_add(ar, ai, xr, xi, br, bi):
    return ar * xr - ai * xi + br, ar * xi + ai * xr + bi


def _short_conv(bch, cw_ref, cb_ref, zbuf_ref):
    tt = bch.shape[0]
    z = bch[:, D_CONV:2 * D_CONV] * bch[:, 2 * D_CONV:3 * D_CONV]
    zbuf_ref[SUBLANES:SUBLANES + tt, :] = z
    z1 = zbuf_ref[SUBLANES - 1:SUBLANES - 1 + tt, :]
    z2 = zbuf_ref[SUBLANES - 2:SUBLANES - 2 + tt, :]
    y = cw_ref[0:1, :] * z2 + cw_ref[1:2, :] * z1 + cw_ref[2:3, :] * z + cb_ref[...]
    zbuf_ref[0:SUBLANES, :] = zbuf_ref[tt:tt + SUBLANES, :]
    return (bch[:, 0:D_CONV] * y).astype(BF16)


def _s5_scan(u, bbd_ref, astep_ref, aseg_ref, apow_ref, cbd_ref, st_ref, uh_ref, up_ref, xl_ref, xs_ref, side_work):
    for h in range(D_SSM // LANES):
        uh_ref[h] = u[:, h * LANES:(h + 1) * LANES]
        for r in range(SSM_STEPS):
            up_ref[r * SUBLANES:(r + 1) * SUBLANES, h * LANES:(h + 1) * LANES] = (
                uh_ref[h, pl.ds(r, SUBLANES, stride=SSM_STEPS), :])
    upb = up_ref[...].astype(BF16)
    first_row = lax.broadcasted_iota(jnp.int32, (SUBLANES, LANES), 0) == 0
    for c in range(N_CHUNKS):
        re = slice(c * CHUNK_COLS, c * CHUNK_COLS + LANES)
        im = slice(c * CHUNK_COLS + LANES, (c + 1) * CHUNK_COLS)
        bu = _dot(upb, jnp.concatenate([bbd_ref[2 * c], bbd_ref[2 * c + 1]], axis=1))
        side_work(c)
        ar, ai = astep_ref[:, re], astep_ref[:, im]
        xr, xi = bu[0:SUBLANES, 0:LANES], bu[0:SUBLANES, LANES:CHUNK_COLS]
        xl_ref[0:SUBLANES, re] = xr
        xl_ref[0:SUBLANES, im] = xi
        for r in range(1, SSM_STEPS):
            rows = slice(r * SUBLANES, (r + 1) * SUBLANES)
            xr, xi = _cmul_add(ar, ai, xr, xi, bu[rows, 0:LANES], bu[rows, LANES:CHUNK_COLS])
            xl_ref[rows, re] = xr
            xl_ref[rows, im] = xi
        sr = jnp.where(first_row, st_ref[:, re], pltpu.roll(xr, 1, 0))
        si = jnp.where(first_row, st_ref[:, im], pltpu.roll(xi, 1, 0))
        for s, shift in enumerate((1, 2, 4)):
            rows = slice(s * SUBLANES, (s + 1) * SUBLANES)
            sr, si = _cmul_add(aseg_ref[rows, re], aseg_ref[rows, im],
                               pltpu.roll(sr, shift, 0), pltpu.roll(si, shift, 0), sr, si)
        rows = slice(3 * SUBLANES, 4 * SUBLANES)
        nr, ni = _cmul_add(aseg_ref[rows, re], aseg_ref[rows, im], sr, si, xr, xi)
        st_ref[:, re] = pltpu.roll(nr, 1, 0)
        st_ref[:, im] = pltpu.roll(ni, 1, 0)
        sr2 = jnp.concatenate([sr, sr], axis=0)
        si2 = jnp.concatenate([si, si], axis=0)
        for k in range(SSM_STEPS // 2):
            rows = slice(2 * k * SUBLANES, 2 * (k + 1) * SUBLANES)
            pr, pi = (jnp.concatenate([jnp.broadcast_to(apow_ref[2 * c + part, 2 * k + i:2 * k + i + 1, :],
                                                        (SUBLANES, LANES)) for i in range(2)], axis=0)
                      for part in range(2))
            fr, fi = _cmul_add(pr, pi, sr2, si2, xl_ref[rows, re], xl_ref[rows, im])
            xs_ref[rows, re] = fr.astype(BF16)
            xs_ref[rows, im] = fi.astype(BF16)
    return _dot(xs_ref[...], cbd_ref[...])


def _s5_finish(yp, u, d_ref, wglu_ref, yh_ref):
    halves = D_SSM // LANES
    for h in range(halves):
        for r in range(SSM_STEPS):
            yh_ref[h, pl.ds(r, SUBLANES, stride=SSM_STEPS), :] = (
                yp[r * SUBLANES:(r + 1) * SUBLANES, h * LANES:(h + 1) * LANES])
    y = jnp.concatenate([yh_ref[h] for h in range(halves)], axis=-1) + d_ref[...] * u
    y = jax.nn.gelu(y)
    y = y * jax.nn.sigmoid(_dot(y.astype(BF16), wglu_ref[...]))
    return y.astype(BF16)


def _swa_heads(q, k_all, v_all, mask, h, sink_ref, layer):
    dn = (((1,), (1,)), ((), ()))
    hs = slice(h * HEAD_DIM, (h + 1) * HEAD_DIM)
    heads = range(h * Q_PER_KV, (h + 1) * Q_PER_KV)
    qh = jnp.concatenate([q[:, j * HEAD_DIM:(j + 1) * HEAD_DIM] for j in heads], axis=0)
    s = lax.dot_general(qh, k_all[:, hs], dn, preferred_element_type=F32)
    probs, denoms = [], []
    for g, j in enumerate(heads):
        sink = sink_ref[layer, j]
        sg = jnp.where(mask, s[g * WINDOW:(g + 1) * WINDOW], -jnp.inf)
        m = jnp.maximum(jnp.max(sg, axis=-1, keepdims=True), sink)
        p = jnp.exp(sg - m)
        denoms.append(jnp.sum(p, axis=-1, keepdims=True) + jnp.exp(sink - m))
        probs.append(p.astype(BF16))
    o = _dot(jnp.concatenate(probs, axis=0), v_all[:, hs])
    return jnp.concatenate([o[g * WINDOW:(g + 1) * WINDOW] / denoms[g] for g in range(Q_PER_KV)],
                           axis=-1).astype(BF16)


def _mixer_kernel(sink_ref, x_ref, win_ref, wglu_ref, wc_ref, ws_ref, wa_ref, wo_ref, cw_ref, cb_ref, bbd_ref,
                  astep_ref, aseg_ref, apow_ref, cbd_ref, d_ref, g_ref, b_ref, *rest, layer, cargo_scales):
    n_cargo = len(cargo_scales)
    o_ref = rest[n_cargo]
    zbuf_ref, st_ref, uh_ref, up_ref, xl_ref, xs_ref, yh_ref, kv_ref, gates_ref = rest[2 * n_cargo + 1:]
    first_tile = pl.program_id(1) == 0

    @pl.when(first_tile)
    def _():
        zbuf_ref[0:SUBLANES, :] = jnp.zeros((SUBLANES, D_CONV), F32)
        st_ref[...] = jnp.zeros(st_ref.shape, F32)
        kv_ref[...] = jnp.zeros(kv_ref.shape, BF16)

    row = lax.broadcasted_iota(jnp.int32, (WINDOW, 2 * WINDOW), 0)
    col = lax.broadcasted_iota(jnp.int32, (WINDOW, 2 * WINDOW), 1)
    band = jnp.logical_and(col > row, col - WINDOW <= row)
    first_band = jnp.logical_and(band, jnp.logical_or(col >= WINDOW, jnp.logical_not(first_tile)))

    tt = SSM_TILE
    for sub in range(x_ref.shape[0] // tt):
        tok = slice(sub * tt, (sub + 1) * tt)
        x = x_ref[tok, :]
        xb = x.astype(BF16)

        def proj(lo, hi):
            return _dot(xb, win_ref[:, lo:hi])

        def gate_block(i):
            cols = slice(i * GATE_COLS, (i + 1) * GATE_COLS)
            gates_ref[sub, :, cols] = jax.nn.sigmoid(proj(OFF_G + i * GATE_COLS, OFF_G + (i + 1) * GATE_COLS))

        pending = list(range(N_BRANCH * D_MODEL // GATE_COLS))

        def gates(n):
            for _ in range(min(n, len(pending))):
                gate_block(pending.pop(0))

        u = proj(OFF_U, OFF_Q)
        y_conv = _short_conv(proj(0, OFF_U), cw_ref, cb_ref, zbuf_ref)
        q = (proj(OFF_Q, OFF_K) * (HEAD_DIM ** -0.5)).astype(BF16)
        k = jnp.concatenate([kv_ref[0], proj(OFF_K, OFF_V).astype(BF16)], axis=0)
        v = jnp.concatenate([kv_ref[1], proj(OFF_V, OFF_G).astype(BF16)], axis=0)
        kv_ref[0] = k[tt:tt + WINDOW]
        kv_ref[1] = v[tt:tt + WINDOW]
        attn = {}

        def attention_unit(i):
            j, h = divmod(i, N_KV_HEADS)
            attn[i] = _swa_heads(q[j * WINDOW:(j + 1) * WINDOW], k[j * WINDOW:(j + 2) * WINDOW],
                                 v[j * WINDOW:(j + 2) * WINDOW], band if (sub or j) else first_band, h,
                                 sink_ref, layer)

        def side_work(c):
            if c % 2 == 0:
                attention_unit(c // 2)
            else:
                gates(GATES_PER_CHUNK)

        gates(GATES_AHEAD_OF_SCAN)
        yp = _s5_scan(u, bbd_ref, astep_ref, aseg_ref, apow_ref, cbd_ref, st_ref, uh_ref, up_ref, xl_ref, xs_ref,
                      side_work)
        y_attn = jnp.concatenate(
            [jnp.concatenate([attn[j * N_KV_HEADS + h] for h in range(N_KV_HEADS)], axis=-1)
             for j in range(tt // WINDOW)], axis=0)
        gates(len(pending))

        y_ssm = _s5_finish(yp, u, d_ref, wglu_ref, yh_ref)

        if sub == 0:
            _narrow_cargo(rest[:n_cargo], rest[n_cargo + 1:2 * n_cargo + 1], cargo_scales)

        t_conv = _dot(y_conv, wc_ref[...])
        t_attn = _dot(y_attn, wa_ref[...])
        t_ssm = _dot(y_ssm, ws_ref[...])
        merged = (gates_ref[sub, :, 0:D_MODEL] * t_conv + gates_ref[sub, :, D_MODEL:2 * D_MODEL] * t_ssm
                  + gates_ref[sub, :, 2 * D_MODEL:3 * D_MODEL] * t_attn)
        mix = _dot(merged.astype(BF16), wo_ref[...])
        o_ref[tok, :] = _layer_norm(ALPHA * x + mix, g_ref[...], b_ref[...])


def _mixer(x, layer, batch, sinks, narrow_w, tables, cargo=()):
    n = x.shape[0]
    tt = SSM_TILE
    tile = MIX_SUBTILES * tt
    per_b = n // batch // tile
    row = pl.BlockSpec((tile, D_MODEL), lambda b, t: (b * per_b + t, 0))
    c_in, c_out, c_shapes = _cargo_specs(cargo, layer, batch * per_b, lambda b, t: b * per_b + t)
    out = pl.pallas_call(
        functools.partial(_mixer_kernel, layer=layer, cargo_scales=tuple(sc for _, sc in cargo)),
        grid=(batch, per_b),
        in_specs=[pl.BlockSpec(memory_space=pltpu.SMEM), row] + [_resident_spec(w) for w in narrow_w]
        + [_resident_spec(w, layer) for w in tables] + c_in,
        out_specs=[row] + c_out,
        out_shape=[jax.ShapeDtypeStruct((n, D_MODEL), F32)] + c_shapes,
        scratch_shapes=[pltpu.VMEM((tt + SUBLANES, D_CONV), F32),
                        pltpu.VMEM((SUBLANES, 2 * N_STATE), F32),
                        pltpu.VMEM((D_SSM // LANES, tt, LANES), F32),
                        pltpu.VMEM((tt, D_SSM), F32),
                        pltpu.VMEM((tt, 2 * N_STATE), F32),
                        pltpu.VMEM((tt, 2 * N_STATE), BF16),
                        pltpu.VMEM((D_SSM // LANES, tt, LANES), F32),
                        pltpu.VMEM((2, WINDOW, D_KV), BF16),
                        pltpu.VMEM((MIX_SUBTILES, tt, N_BRANCH * D_MODEL), F32)],
        compiler_params=_params("arbitrary", "arbitrary"),
        name="mixer",
    )(sinks, x, *narrow_w, *tables, *[a for a, _ in cargo])
    return out[0], out[1:]


def _ssm_tables(a_re, a_im, log_dt, b_re, b_im, c_re, c_im):
    dt = jnp.exp(log_dt)[..., None]
    mag = jnp.exp(a_re * dt)
    ang = a_im * dt
    abar_re = mag * jnp.cos(ang)
    abar_im = mag * jnp.sin(ang)
    nr = abar_re - 1.0
    ni = abar_im
    den = a_re * a_re + a_im * a_im
    coef_re = (nr * a_re + ni * a_im) / den
    coef_im = (ni * a_re - nr * a_im) / den
    bbar_re = coef_re[..., None] * b_re - coef_im[..., None] * b_im
    bbar_im = coef_re[..., None] * b_im + coef_im[..., None] * b_re

    chunk = jnp.arange(N_CHUNKS)[:, None]
    lane_group = GROUPS_PER_CHUNK * chunk + jnp.arange(LANES)[None, :] // SSM_STATE
    chan_group = jnp.arange(D_SSM) // SSM_GROUP
    b_lane = jnp.stack([bbar_re, bbar_im], axis=1).transpose(0, 1, 2, 4, 3)
    b_lane = jnp.tile(b_lane.reshape(DEPTH, 2, D_SSM, SSM_STATE), (1, 1, 1, GROUPS_PER_CHUNK))
    sel_b = (chan_group[None, :, None] == lane_group[:, None, :]).astype(F32)
    bbd = (b_lane[:, None] * sel_b[None, :, None]).reshape(DEPTH, 2 * N_CHUNKS, D_SSM, LANES).astype(BF16)
    c_chan = jnp.stack([c_re, -c_im], axis=1).transpose(0, 1, 4, 2, 3).reshape(DEPTH, 2, SSM_STATE, D_SSM)
    c_chan = jnp.tile(c_chan, (1, 1, GROUPS_PER_CHUNK, 1))
    sel_c = (lane_group[:, :, None] == chan_group[None, None, :]).astype(F32)
    cbd = (c_chan[:, None] * sel_c[None, :, None]).reshape(DEPTH, 2 * N_STATE, D_SSM).astype(BF16)

    pr, pi = abar_re.reshape(DEPTH, 1, N_STATE), abar_im.reshape(DEPTH, 1, N_STATE)
    while pr.shape[1] < SSM_STEPS:
        tr, ti = pr[:, -1:], pi[:, -1:]
        pr, pi = (jnp.concatenate([pr, pr * tr - pi * ti], axis=1),
                  jnp.concatenate([pi, pr * ti + pi * tr], axis=1))

    def by_chunk(re, im):
        rows = re.shape[1]
        return jnp.stack([re.reshape(DEPTH, rows, N_CHUNKS, LANES), im.reshape(DEPTH, rows, N_CHUNKS, LANES)], axis=3)

    powers = by_chunk(pr, pi)
    apow = powers.transpose(0, 2, 3, 1, 4).reshape(DEPTH, 2 * N_CHUNKS, SSM_STEPS, LANES)
    astep = jnp.broadcast_to(powers[:, 0:1], (DEPTH, SUBLANES) + powers.shape[2:]).reshape(DEPTH, SUBLANES, 2 * N_STATE)

    sublane = jnp.arange(SUBLANES)[None, :, None]
    qr, qi = pr[:, -1:], pi[:, -1:]
    seg_re, seg_im = [], []
    for shift in (1, 2, 4):
        seg_re.append(jnp.where(sublane >= shift, qr, 0.0))
        seg_im.append(jnp.where(sublane >= shift, qi, 0.0))
        qr, qi = qr * qr - qi * qi, 2.0 * qr * qi
    seg_re.append(jnp.broadcast_to(pr[:, -1:], (DEPTH, SUBLANES, N_STATE)))
    seg_im.append(jnp.broadcast_to(pi[:, -1:], (DEPTH, SUBLANES, N_STATE)))
    aseg = by_chunk(jnp.concatenate(seg_re, axis=1), jnp.concatenate(seg_im, axis=1))
    return bbd, astep, aseg.reshape(DEPTH, 4 * SUBLANES, 2 * N_STATE), apow, cbd


def _vec(p):
    return p.reshape(DEPTH, 1, -1)


MIXER_NARROW = ('w_in', 'ssm_w_glu', 'w_br_conv', 'w_br_ssm', 'w_br_attn', 'w_out')


def _mixer_tables(p):
    bbd, astep, aseg, apow, cbd = _ssm_tables(p['ssm_a_re'], p['ssm_a_im'], p['ssm_log_dt'], p['ssm_b_re'],
                                              p['ssm_b_im'], p['ssm_c_re'], p['ssm_c_im'])
    return (p['conv_w'], _vec(p['conv_b']), bbd, astep, aseg, apow, cbd, _vec(p['ssm_d']),
            _vec(p['ln2_g']), _vec(p['ln2_b']))


def kernel(x, ffn1_w_gate, ffn1_w_up, ffn1_w_down, ln1_g, ln1_b, w_in, conv_w, conv_b, ssm_a_re, ssm_a_im, ssm_log_dt, ssm_b_re, ssm_b_im, ssm_c_re, ssm_c_im, ssm_d, ssm_w_glu, attn_sinks, w_br_conv, w_br_ssm, w_br_attn, w_out, ln2_g, ln2_b, ffn2_w_gate, ffn2_w_up, ffn2_w_down, ln3_g, ln3_b):
    batch, seq, _ = x.shape
    h = x.reshape(batch * seq, D_MODEL)
    p = dict(w_in=w_in, conv_w=conv_w, conv_b=conv_b, ssm_a_re=ssm_a_re, ssm_a_im=ssm_a_im, ssm_log_dt=ssm_log_dt,
             ssm_b_re=ssm_b_re, ssm_b_im=ssm_b_im, ssm_c_re=ssm_c_re, ssm_c_im=ssm_c_im, ssm_d=ssm_d,
             ssm_w_glu=ssm_w_glu, w_br_conv=w_br_conv, w_br_ssm=w_br_ssm, w_br_attn=w_br_attn, w_out=w_out,
             ln2_g=ln2_g, ln2_b=ln2_b)
    tables = _mixer_tables(p)
    mixer_cargo = [(p[name], 1.0) for name in MIXER_NARROW]
    ffn1_cargo = [(ffn1_w_gate, 1.0), (ffn1_w_up, 1.0), (ffn1_w_down, 0.5)]
    ffn2_cargo = [(ffn2_w_gate, 1.0), (ffn2_w_up, 1.0), (ffn2_w_down, 0.5)]
    ffn_w = (ffn1_w_gate[0].astype(BF16), ffn1_w_up[0].astype(BF16), (0.5 * ffn1_w_down[0]).astype(BF16))
    ln1, ln3 = (_vec(ln1_g), _vec(ln1_b)), (_vec(ln3_g), _vec(ln3_b))
    for l in range(DEPTH):
        h, mix_w = _ffn_ln(h, l, *ffn_w, *ln1, cargo=mixer_cargo, cargo_layer=l)
        h, ffn_w = _mixer(h, l, batch, attn_sinks, mix_w, tables, cargo=ffn2_cargo)
        last = l == DEPTH - 1
        h, ffn_w = _ffn_ln(h, l, *ffn_w, *ln3, cargo=() if last else ffn1_cargo, cargo_layer=l + 1)
    return h.reshape(batch, seq, D_MODEL)
```

```python
import functools

import jax
import jax.numpy as jnp
from jax import lax
from jax.experimental import pallas as pl
from jax.experimental.pallas import tpu as pltpu

D_MODEL = 1024
DEPTH = 4
D_CONV = 256
CONV_WIDTH = 3
D_SSM = 256
SSM_GROUP = 16
N_SSM_GROUPS = D_SSM // SSM_GROUP
SSM_STATE = 64
N_STATE = N_SSM_GROUPS * SSM_STATE
N_Q_HEADS = 8
N_KV_HEADS = 2
Q_PER_KV = N_Q_HEADS // N_KV_HEADS
HEAD_DIM = 64
D_ATTN = N_Q_HEADS * HEAD_DIM
D_KV = N_KV_HEADS * HEAD_DIM
WINDOW = 128
N_BRANCH = 3
D_FF = 2816
ALPHA = (2 * DEPTH) ** 0.25
LN_EPS = 1e-5
LOG2_E = 1.4426950408889634
D_IN = 3 * D_CONV + D_SSM + D_ATTN + 2 * D_KV + N_BRANCH * D_MODEL

OFF_U = 3 * D_CONV
OFF_Q = OFF_U + D_SSM
OFF_K = OFF_Q + D_ATTN
OFF_V = OFF_K + D_KV
OFF_G = OFF_V + D_KV

SUBLANES = 8
LANES = 128
BF16_ROWS = 16
FF_CHUNK = 256
FFN_ROWS = 512
DOWN_ROWS = 256
VMEM_LIMIT = 56 * 1024 * 1024

SSM_TILE = 256
SSM_STEPS = SSM_TILE // SUBLANES
N_CHUNKS = N_STATE // LANES
CHUNK_COLS = 2 * LANES
GROUPS_PER_CHUNK = LANES // SSM_STATE
GATE_COLS = 256
MIX_SUBTILES = 2
GATES_AHEAD_OF_SCAN = 1
GATES_PER_CHUNK = 1

F32 = jnp.float32
BF16 = jnp.bfloat16


def _dot(a, b):
    return jnp.dot(a, b, preferred_element_type=F32)


def _layer_norm(r, g, b):
    mu = jnp.mean(r, axis=-1, keepdims=True)
    d = r - mu
    var = jnp.mean(d * d, axis=-1, keepdims=True)
    return d * lax.rsqrt(var + LN_EPS) * g + b


def _resident_spec(arr, layer=None):
    if layer is None:
        return pl.BlockSpec(arr.shape, lambda *_: (0,) * arr.ndim, pipeline_mode=pl.Buffered(1))
    zeros = (0,) * (arr.ndim - 1)
    return pl.BlockSpec((None,) + arr.shape[1:], lambda *_: (layer,) + zeros, pipeline_mode=pl.Buffered(1))


def _cargo_specs(cargo, layer, steps, step_of):
    in_specs, out_specs, out_shapes = [], [], []
    for arr, _ in cargo:
        _, rows, cols = arr.shape
        blocks = steps if rows % (steps * BF16_ROWS) == 0 else steps // 2
        assert rows % (blocks * BF16_ROWS) == 0 and steps % blocks == 0, (arr.shape, steps)
        per = steps // blocks
        in_specs.append(pl.BlockSpec((None, rows // blocks, cols),
                                     lambda *ids, per=per: (layer, step_of(*ids) // per, 0)))
        out_specs.append(pl.BlockSpec((rows // blocks, cols), lambda *ids, per=per: (step_of(*ids) // per, 0)))
        out_shapes.append(jax.ShapeDtypeStruct((rows, cols), BF16))
    return in_specs, out_specs, out_shapes


def _narrow_cargo(in_refs, out_refs, scales):
    for src, dst, scale in zip(in_refs, out_refs, scales):
        w = src[...]
        dst[...] = (w if scale == 1.0 else w * scale).astype(BF16)


def _params(*sem):
    return pltpu.CompilerParams(dimension_semantics=sem, vmem_limit_bytes=VMEM_LIMIT)


def _ffn_kernel(x_ref, wg_ref, wu_ref, wd_ref, g_ref, b_ref, *rest, cargo_scales):
    n_cargo = len(cargo_scales)
    o_ref = rest[n_cargo]
    h_ref = rest[-1]
    n_chunks = D_FF // FF_CHUNK
    xb_all = x_ref[...].astype(BF16)
    for r in range(x_ref.shape[0] // FFN_ROWS):
        rows = slice(r * FFN_ROWS, (r + 1) * FFN_ROWS)
        xb = xb_all[rows, :]
        for c in range(n_chunks):
            cols = slice(c * FF_CHUNK, (c + 1) * FF_CHUNK)
            gate = _dot(xb, wg_ref[:, cols])
            up = _dot(xb, wu_ref[:, cols])
            h_ref[r, :, cols] = (jax.nn.silu(gate) * up).astype(BF16)
            if r == 0 and c == n_chunks // 2:
                _narrow_cargo(rest[:n_cargo], rest[n_cargo + 1:2 * n_cargo + 1], cargo_scales)
        for rb in range(FFN_ROWS // DOWN_ROWS):
            blk = slice(r * FFN_ROWS + rb * DOWN_ROWS, r * FFN_ROWS + (rb + 1) * DOWN_ROWS)
            acc = _dot(h_ref[r, rb * DOWN_ROWS:(rb + 1) * DOWN_ROWS, :], wd_ref[...])
            o_ref[blk, :] = _layer_norm(ALPHA * x_ref[blk, :] + acc, g_ref[...], b_ref[...])


def _ffn_ln(x, layer, wg, wu, wd, g, b, cargo=(), cargo_layer=0, tm=2 * FFN_ROWS):
    n = x.shape[0]
    steps = n // tm
    row = pl.BlockSpec((tm, D_MODEL), lambda i: (i, 0))
    c_in, c_out, c_shapes = _cargo_specs(cargo, cargo_layer, steps, lambda i: i)
    out = pl.pallas_call(
        functools.partial(_ffn_kernel, cargo_scales=tuple(sc for _, sc in cargo)),
        grid=(steps,),
        in_specs=[row, _resident_spec(wg), _resident_spec(wu), _resident_spec(wd), _resident_spec(g, layer),
                  _resident_spec(b, layer)] + c_in,
        out_specs=[row] + c_out,
        out_shape=[jax.ShapeDtypeStruct((n, D_MODEL), F32)] + c_shapes,
        scratch_shapes=[pltpu.VMEM((tm // FFN_ROWS, FFN_ROWS, D_FF), BF16)],
        compiler_params=_params("parallel"),
        name="ffn_ln",
    )(x, wg, wu, wd, g, b, *[a for a, _ in cargo])
    return out[0], out[1:]


def _cmul_add(ar, ai, xr, xi, br, bi):
    return ar * xr - ai * xi + br, ar * xi + ai * xr + bi


def _short_conv(bch, cw_ref, cb_ref, zbuf_ref):
    tt = bch.shape[0]
    z = bch[:, D_CONV:2 * D_CONV] * bch[:, 2 * D_CONV:3 * D_CONV]
    zbuf_ref[SUBLANES:SUBLANES + tt, :] = z
    z1 = zbuf_ref[SUBLANES - 1:SUBLANES - 1 + tt, :]
    z2 = zbuf_ref[SUBLANES - 2:SUBLANES - 2 + tt, :]
    y = cw_ref[0:1, :] * z2 + cw_ref[1:2, :] * z1 + cw_ref[2:3, :] * z + cb_ref[...]
    zbuf_ref[0:SUBLANES, :] = zbuf_ref[tt:tt + SUBLANES, :]
    return (bch[:, 0:D_CONV] * y).astype(BF16)


def _s5_scan(u, bbd_ref, astep_ref, aseg_ref, apow_ref, cbd_ref, st_ref, uh_ref, up_ref, xl_ref, xs_ref, side_work):
    for h in range(D_SSM // LANES):
        uh_ref[h] = u[:, h * LANES:(h + 1) * LANES]
        for r in range(SSM_STEPS):
            up_ref[r * SUBLANES:(r + 1) * SUBLANES, h * LANES:(h + 1) * LANES] = (
                uh_ref[h, pl.ds(r, SUBLANES, stride=SSM_STEPS), :])
    upb = up_ref[...].astype(BF16)
    first_row = lax.broadcasted_iota(jnp.int32, (SUBLANES, LANES), 0) == 0
    for c in range(N_CHUNKS):
        re = slice(c * CHUNK_COLS, c * CHUNK_COLS + LANES)
        im = slice(c * CHUNK_COLS + LANES, (c + 1) * CHUNK_COLS)
        bu = _dot(upb, jnp.concatenate([bbd_ref[2 * c], bbd_ref[2 * c + 1]], axis=1))
        side_work(c)
        ar, ai = astep_ref[:, re], astep_ref[:, im]
        xr, xi = bu[0:SUBLANES, 0:LANES], bu[0:SUBLANES, LANES:CHUNK_COLS]
        xl_ref[0:SUBLANES, re] = xr
        xl_ref[0:SUBLANES, im] = xi
        for r in range(1, SSM_STEPS):
            rows = slice(r * SUBLANES, (r + 1) * SUBLANES)
            xr, xi = _cmul_add(ar, ai, xr, xi, bu[rows, 0:LANES], bu[rows, LANES:CHUNK_COLS])
            xl_ref[rows, re] = xr
            xl_ref[rows, im] = xi
        sr = jnp.where(first_row, st_ref[:, re], pltpu.roll(xr, 1, 0))
        si = jnp.where(first_row, st_ref[:, im], pltpu.roll(xi, 1, 0))
        for s, shift in enumerate((1, 2, 4)):
            rows = slice(s * SUBLANES, (s + 1) * SUBLANES)
            sr, si = _cmul_add(aseg_ref[rows, re], aseg_ref[rows, im],
                               pltpu.roll(sr, shift, 0), pltpu.roll(si, shift, 0), sr, si)
        rows = slice(3 * SUBLANES, 4 * SUBLANES)
        nr, ni = _cmul_add(aseg_ref[rows, re], aseg_ref[rows, im], sr, si, xr, xi)
        st_ref[:, re] = pltpu.roll(nr, 1, 0)
        st_ref[:, im] = pltpu.roll(ni, 1, 0)
        sr2 = jnp.concatenate([sr, sr], axis=0)
        si2 = jnp.concatenate([si, si], axis=0)
        for k in range(SSM_STEPS // 2):
            rows = slice(2 * k * SUBLANES, 2 * (k + 1) * SUBLANES)
            pr, pi = (jnp.concatenate([jnp.broadcast_to(apow_ref[2 * c + part, 2 * k + i:2 * k + i + 1, :],
                                                        (SUBLANES, LANES)) for i in range(2)], axis=0)
                      for part in range(2))
            fr, fi = _cmul_add(pr, pi, sr2, si2, xl_ref[rows, re], xl_ref[rows, im])
            xs_ref[rows, re] = fr.astype(BF16)
            xs_ref[rows, im] = fi.astype(BF16)
    return _dot(xs_ref[...], cbd_ref[...])


def _s5_finish(yp, u, d_ref, wglu_ref, yh_ref):
    halves = D_SSM // LANES
    for h in range(halves):
        for r in range(SSM_STEPS):
            yh_ref[h, pl.ds(r, SUBLANES, stride=SSM_STEPS), :] = (
                yp[r * SUBLANES:(r + 1) * SUBLANES, h * LANES:(h + 1) * LANES])
    y = jnp.concatenate([yh_ref[h] for h in range(halves)], axis=-1) + d_ref[...] * u
    y = jax.nn.gelu(y)
    y = y * jax.nn.sigmoid(_dot(y.astype(BF16), wglu_ref[...]))
    return y.astype(BF16)


def _swa_heads(q, k_all, v_all, mask, h, sink_ref, layer):
    dn = (((1,), (1,)), ((), ()))
    hs = slice(h * HEAD_DIM, (h + 1) * HEAD_DIM)
    heads = range(h * Q_PER_KV, (h + 1) * Q_PER_KV)
    qh = jnp.concatenate([q[:, j * HEAD_DIM:(j + 1) * HEAD_DIM] for j in heads], axis=0)
    s = lax.dot_general(qh, k_all[:, hs], dn, preferred_element_type=F32)
    probs, denoms = [], []
    for g, j in enumerate(heads):
        sink = sink_ref[layer, j] * LOG2_E
        sg = jnp.where(mask, s[g * WINDOW:(g + 1) * WINDOW], -jnp.inf)
        m = jnp.maximum(jnp.max(sg, axis=-1, keepdims=True), sink)
        p = jnp.exp2(sg - m)
        denoms.append(jnp.sum(p, axis=-1, keepdims=True) + jnp.exp2(sink - m))
        probs.append(p.astype(BF16))
    o = _dot(jnp.concatenate(probs, axis=0), v_all[:, hs])
    return jnp.concatenate([o[g * WINDOW:(g + 1) * WINDOW] / denoms[g] for g in range(Q_PER_KV)],
                           axis=-1).astype(BF16)


def _mixer_kernel(sink_ref, x_ref, win_ref, wglu_ref, wc_ref, ws_ref, wa_ref, wo_ref, cw_ref, cb_ref, bbd_ref,
                  astep_ref, aseg_ref, apow_ref, cbd_ref, d_ref, g_ref, b_ref, *rest, layer, cargo_scales):
    n_cargo = len(cargo_scales)
    o_ref = rest[n_cargo]
    zbuf_ref, st_ref, uh_ref, up_ref, xl_ref, xs_ref, yh_ref, kv_ref, gates_ref = rest[2 * n_cargo + 1:]
    first_tile = pl.program_id(1) == 0

    @pl.when(first_tile)
    def _():
        zbuf_ref[0:SUBLANES, :] = jnp.zeros((SUBLANES, D_CONV), F32)
        st_ref[...] = jnp.zeros(st_ref.shape, F32)
        kv_ref[...] = jnp.zeros(kv_ref.shape, BF16)

    row = lax.broadcasted_iota(jnp.int32, (WINDOW, 2 * WINDOW), 0)
    col = lax.broadcasted_iota(jnp.int32, (WINDOW, 2 * WINDOW), 1)
    band = jnp.logical_and(col > row, col - WINDOW <= row)
    first_band = jnp.logical_and(band, jnp.logical_or(col >= WINDOW, jnp.logical_not(first_tile)))

    tt = SSM_TILE
    for sub in range(x_ref.shape[0] // tt):
        tok = slice(sub * tt, (sub + 1) * tt)
        x = x_ref[tok, :]
        xb = x.astype(BF16)

        def proj(lo, hi):
            return _dot(xb, win_ref[:, lo:hi])

        def gate_block(i):
            cols = slice(i * GATE_COLS, (i + 1) * GATE_COLS)
            gates_ref[sub, :, cols] = jax.nn.sigmoid(proj(OFF_G + i * GATE_COLS, OFF_G + (i + 1) * GATE_COLS))

        pending = list(range(N_BRANCH * D_MODEL // GATE_COLS))

        def gates(n):
            for _ in range(min(n, len(pending))):
                gate_block(pending.pop(0))

        u = proj(OFF_U, OFF_Q)
        y_conv = _short_conv(proj(0, OFF_U), cw_ref, cb_ref, zbuf_ref)
        q = (proj(OFF_Q, OFF_K) * (HEAD_DIM ** -0.5 * LOG2_E)).astype(BF16)
        k = jnp.concatenate([kv_ref[0], proj(OFF_K, OFF_V).astype(BF16)], axis=0)
        v = jnp.concatenate([kv_ref[1], proj(OFF_V, OFF_G).astype(BF16)], axis=0)
        kv_ref[0] = k[tt:tt + WINDOW]
        kv_ref[1] = v[tt:tt + WINDOW]
        attn = {}

        def attention_unit(i):
            j, h = divmod(i, N_KV_HEADS)
            attn[i] = _swa_heads(q[j * WINDOW:(j + 1) * WINDOW], k[j * WINDOW:(j + 2) * WINDOW],
                                 v[j * WINDOW:(j + 2) * WINDOW], band if (sub or j) else first_band, h,
                                 sink_ref, layer)

        def side_work(c):
            if c % 2 == 0:
                attention_unit(c // 2)
            else:
                gates(GATES_PER_CHUNK)

        gates(GATES_AHEAD_OF_SCAN)
        yp = _s5_scan(u, bbd_ref, astep_ref, aseg_ref, apow_ref, cbd_ref, st_ref, uh_ref, up_ref, xl_ref, xs_ref,
                      side_work)
        y_attn = jnp.concatenate(
            [jnp.concatenate([attn[j * N_KV_HEADS + h] for h in range(N_KV_HEADS)], axis=-1)
             for j in range(tt // WINDOW)], axis=0)
        gates(len(pending))

        y_ssm = _s5_finish(yp, u, d_ref, wglu_ref, yh_ref)

        if sub == 0:
            _narrow_cargo(rest[:n_cargo], rest[n_cargo + 1:2 * n_cargo + 1], cargo_scales)

        t_conv = _dot(y_conv, wc_ref[...])
        t_attn = _dot(y_attn, wa_ref[...])
        t_ssm = _dot(y_ssm, ws_ref[...])
        merged = (gates_ref[sub, :, 0:D_MODEL] * t_conv + gates_ref[sub, :, D_MODEL:2 * D_MODEL] * t_ssm
                  + gates_ref[sub, :, 2 * D_MODEL:3 * D_MODEL] * t_attn)
        mix = _dot(merged.astype(BF16), wo_ref[...])
        o_ref[tok, :] = _layer_norm(ALPHA * x + mix, g_ref[...], b_ref[...])


def _mixer(x, layer, batch, sinks, narrow_w, tables, cargo=()):
    n = x.shape[0]
    tt = SSM_TILE
    tile = MIX_SUBTILES * tt
    per_b = n // batch // tile
    row = pl.BlockSpec((tile, D_MODEL), lambda b, t: (b * per_b + t, 0))
    c_in, c_out, c_shapes = _cargo_specs(cargo, layer, batch * per_b, lambda b, t: b * per_b + t)
    out = pl.pallas_call(
        functools.partial(_mixer_kernel, layer=layer, cargo_scales=tuple(sc for _, sc in cargo)),
        grid=(batch, per_b),
        in_specs=[pl.BlockSpec(memory_space=pltpu.SMEM), row] + [_resident_spec(w) for w in narrow_w]
        + [_resident_spec(w, layer) for w in tables] + c_in,
        out_specs=[row] + c_out,
        out_shape=[jax.ShapeDtypeStruct((n, D_MODEL), F32)] + c_shapes,
        scratch_shapes=[pltpu.VMEM((tt + SUBLANES, D_CONV), F32),
                        pltpu.VMEM((SUBLANES, 2 * N_STATE), F32),
                        pltpu.VMEM((D_SSM // LANES, tt, LANES), F32),
                        pltpu.VMEM((tt, D_SSM), F32),
                        pltpu.VMEM((tt, 2 * N_STATE), F32),
                        pltpu.VMEM((tt, 2 * N_STATE), BF16),
                        pltpu.VMEM((D_SSM // LANES, tt, LANES), F32),
                        pltpu.VMEM((2, WINDOW, D_KV), BF16),
                        pltpu.VMEM((MIX_SUBTILES, tt, N_BRANCH * D_MODEL), F32)],
        compiler_params=_params("arbitrary", "arbitrary"),
        name="mixer",
    )(sinks, x, *narrow_w, *tables, *[a for a, _ in cargo])
    return out[0], out[1:]


def _ssm_tables(a_re, a_im, log_dt, b_re, b_im, c_re, c_im):
    dt = jnp.exp(log_dt)[..., None]
    mag = jnp.exp(a_re * dt)
    ang = a_im * dt
    abar_re = mag * jnp.cos(ang)
    abar_im = mag * jnp.sin(ang)
    nr = abar_re - 1.0
    ni = abar_im
    den = a_re * a_re + a_im * a_im
    coef_re = (nr * a_re + ni * a_im) / den
    coef_im = (ni * a_re - nr * a_im) / den
    bbar_re = coef_re[..., None] * b_re - coef_im[..., None] * b_im
    bbar_im = coef_re[..., None] * b_im + coef_im[..., None] * b_re

    chunk = jnp.arange(N_CHUNKS)[:, None]
    lane_group = GROUPS_PER_CHUNK * chunk + jnp.arange(LANES)[None, :] // SSM_STATE
    chan_group = jnp.arange(D_SSM) // SSM_GROUP
    b_lane = jnp.stack([bbar_re, bbar_im], axis=1).transpose(0, 1, 2, 4, 3)
    b_lane = jnp.tile(b_lane.reshape(DEPTH, 2, D_SSM, SSM_STATE), (1, 1, 1, GROUPS_PER_CHUNK))
    sel_b = (chan_group[None, :, None] == lane_group[:, None, :]).astype(F32)
    bbd = (b_lane[:, None] * sel_b[None, :, None]).reshape(DEPTH, 2 * N_CHUNKS, D_SSM, LANES).astype(BF16)
    c_chan = jnp.stack([c_re, -c_im], axis=1).transpose(0, 1, 4, 2, 3).reshape(DEPTH, 2, SSM_STATE, D_SSM)
    c_chan = jnp.tile(c_chan, (1, 1, GROUPS_PER_CHUNK, 1))
    sel_c = (lane_group[:, :, None] == chan_group[None, None, :]).astype(F32)
    cbd = (c_chan[:, None] * sel_c[None, :, None]).reshape(DEPTH, 2 * N_STATE, D_SSM).astype(BF16)

    pr, pi = abar_re.reshape(DEPTH, 1, N_STATE), abar_im.reshape(DEPTH, 1, N_STATE)
    while pr.shape[1] < SSM_STEPS:
        tr, ti = pr[:, -1:], pi[:, -1:]
        pr, pi = (jnp.concatenate([pr, pr * tr - pi * ti], axis=1),
                  jnp.concatenate([pi, pr * ti + pi * tr], axis=1))

    def by_chunk(re, im):
        rows = re.shape[1]
        return jnp.stack([re.reshape(DEPTH, rows, N_CHUNKS, LANES), im.reshape(DEPTH, rows, N_CHUNKS, LANES)], axis=3)

    powers = by_chunk(pr, pi)
    apow = powers.transpose(0, 2, 3, 1, 4).reshape(DEPTH, 2 * N_CHUNKS, SSM_STEPS, LANES)
    astep = jnp.broadcast_to(powers[:, 0:1], (DEPTH, SUBLANES) + powers.shape[2:]).reshape(DEPTH, SUBLANES, 2 * N_STATE)

    sublane = jnp.arange(SUBLANES)[None, :, None]
    qr, qi = pr[:, -1:], pi[:, -1:]
    seg_re, seg_im = [], []
    for shift in (1, 2, 4):
        seg_re.append(jnp.where(sublane >= shift, qr, 0.0))
        seg_im.append(jnp.where(sublane >= shift, qi, 0.0))
        qr, qi = qr * qr - qi * qi, 2.0 * qr * qi
    seg_re.append(jnp.broadcast_to(pr[:, -1:], (DEPTH, SUBLANES, N_STATE)))
    seg_im.append(jnp.broadcast_to(pi[:, -1:], (DEPTH, SUBLANES, N_STATE)))
    aseg = by_chunk(jnp.concatenate(seg_re, axis=1), jnp.concatenate(seg_im, axis=1))
    return bbd, astep, aseg.reshape(DEPTH, 4 * SUBLANES, 2 * N_STATE), apow, cbd


def _vec(p):
    return p.reshape(DEPTH, 1, -1)


MIXER_NARROW = ('w_in', 'ssm_w_glu', 'w_br_conv', 'w_br_ssm', 'w_br_attn', 'w_out')


def _mixer_tables(p):
    bbd, astep, aseg, apow, cbd = _ssm_tables(p['ssm_a_re'], p['ssm_a_im'], p['ssm_log_dt'], p['ssm_b_re'],
                                              p['ssm_b_im'], p['ssm_c_re'], p['ssm_c_im'])
    return (p['conv_w'], _vec(p['conv_b']), bbd, astep, aseg, apow, cbd, _vec(p['ssm_d']),
            _vec(p['ln2_g']), _vec(p['ln2_b']))


def kernel(x, ffn1_w_gate, ffn1_w_up, ffn1_w_down, ln1_g, ln1_b, w_in, conv_w, conv_b, ssm_a_re, ssm_a_im, ssm_log_dt, ssm_b_re, ssm_b_im, ssm_c_re, ssm_c_im, ssm_d, ssm_w_glu, attn_sinks, w_br_conv, w_br_ssm, w_br_attn, w_out, ln2_g, ln2_b, ffn2_w_gate, ffn2_w_up, ffn2_w_down, ln3_g, ln3_b):
    batch, seq, _ = x.shape
    h = x.reshape(batch * seq, D_MODEL)
    p = dict(w_in=w_in, conv_w=conv_w, conv_b=conv_b, ssm_a_re=ssm_a_re, ssm_a_im=ssm_a_im, ssm_log_dt=ssm_log_dt,
             ssm_b_re=ssm_b_re, ssm_b_im=ssm_b_im, ssm_c_re=ssm_c_re, ssm_c_im=ssm_c_im, ssm_d=ssm_d,
             ssm_w_glu=ssm_w_glu, w_br_conv=w_br_conv, w_br_ssm=w_br_ssm, w_br_attn=w_br_attn, w_out=w_out,
             ln2_g=ln2_g, ln2_b=ln2_b)
    tables = _mixer_tables(p)
    mixer_cargo = [(p[name], 1.0) for name in MIXER_NARROW]
    ffn1_cargo = [(ffn1_w_gate, 1.0), (ffn1_w_up, 1.0), (ffn1_w_down, 0.5)]
    ffn2_cargo = [(ffn2_w_gate, 1.0), (ffn2_w_up, 1.0), (ffn2_w_down, 0.5)]
    ffn_w = (ffn1_w_gate[0].astype(BF16), ffn1_w_up[0].astype(BF16), (0.5 * ffn1_w_down[0]).astype(BF16))
    ln1, ln3 = (_vec(ln1_g), _vec(ln1_b)), (_vec(ln3_g), _vec(ln3_b))
    for l in range(DEPTH):
        h, mix_w = _ffn_ln(h, l, *ffn_w, *ln1, cargo=mixer_cargo, cargo_layer=l)
        h, ffn_w = _mixer(h, l, batch, attn_sinks, mix_w, tables, cargo=ffn2_cargo)
        last = l == DEPTH - 1
        h, ffn_w = _ffn_ln(h, l, *ffn_w, *ln3, cargo=() if last else ffn1_cargo, cargo_layer=l + 1)
    return h.reshape(batch, seq, D_MODEL)
```
